```python
import math
import jax
import jax.numpy as jnp
from jax import lax
import numpy as np

D_MODEL = 1024
BATCH = 1
SEQ = 16384
DEPTH = 2
DEC_BATCH = 8
DEC_SEQ = 4096
PAST_LEN = 128

HEAD_DIM = D_MODEL // 16
A_HEADS = 4
A_NOPE = HEAD_DIM
A_ROPE = HEAD_DIM // 2
A_V = HEAD_DIM
A_Q_LORA = D_MODEL // 4
A_KV_LORA = D_MODEL // 8
B_HEADS = 4
B_DIM = HEAD_DIM
GRID_W = 64
NA_ROWS = 8
NA_COLS = 16
C_HEADS = 4
C_DIM = HEAD_DIM // 2
C_VDIM = HEAD_DIM
D_HEADS = 4
D_DIM = HEAD_DIM
D_BRANCHES = ((128, 1), (512, 4), (2048, 16))
N_EXPERTS = 16
CAP_FACTOR = 2
EXPERT_FF = D_MODEL
QBLK = 128
ROPE_THETA = 10000.0
RMS_EPS = 1e-6
MIX_W = A_HEADS * A_V + B_HEADS * B_DIM + C_HEADS * C_VDIM + D_HEADS * D_DIM
PROJ_SIZES = (A_Q_LORA, A_KV_LORA, A_ROPE,
              B_HEADS * B_DIM, B_HEADS * B_DIM, B_HEADS * B_DIM,
              C_HEADS * 2 * C_DIM, C_HEADS * 2 * C_DIM, C_HEADS * C_VDIM,
              D_HEADS * D_DIM, D_HEADS * D_DIM, D_HEADS * D_DIM)
IN_W = sum(PROJ_SIZES)

kernel_name = "hybrid_bidir_encoder_parallel_groups"


def rmsnorm(x, g):
    xf = x.astype(jnp.float32)
    y = xf * lax.rsqrt(jnp.mean(xf * xf, axis=-1, keepdims=True) + RMS_EPS)
    return (y * g.astype(jnp.float32)).astype(x.dtype)


def rope(x, pos):
    half = x.shape[-1] // 2
    inv = ROPE_THETA ** (-jnp.arange(half, dtype=jnp.float32) / half)
    ang = pos.astype(jnp.float32)[:, None] * inv[None, :]
    shape = (pos.shape[0],) + (1,) * (x.ndim - 3) + (half,)
    cos = jnp.cos(ang).reshape(shape)
    sin = jnp.sin(ang).reshape(shape)
    xf = x.astype(jnp.float32)
    x1, x2 = xf[..., :half], xf[..., half:]
    return jnp.concatenate([x1 * cos - x2 * sin, x2 * cos + x1 * sin], axis=-1).astype(x.dtype)


def split_points():
    pts, s = [], 0
    for z in PROJ_SIZES[:-1]:
        s += z
        pts.append(s)
    return pts


def to_blocks(x):
    b, t = x.shape[:2]
    return jnp.moveaxis(x.reshape((b, t // QBLK, QBLK) + x.shape[2:]), 1, 0)


def from_blocks(y):
    nb, b, q = y.shape[:3]
    return jnp.moveaxis(y, 0, 1).reshape((b, nb * q) + y.shape[3:])


def gathered_attention(qb, k, v, idx, add):
    scale = qb.shape[-1] ** -0.5
    kg = k[:, idx]
    vg = v[:, idx]
    s = jnp.einsum('bqhd,bqkhd->bhqk', qb, kg).astype(jnp.float32) * scale + add
    lse = jax.nn.logsumexp(s, axis=-1)
    p = jnp.exp(s - lse[..., None])
    out = jnp.einsum('bhqk,bqkhd->bqhd', p.astype(v.dtype), vg)
    return out, lse


def mla_attention(q_nope, q_rope, k_nope, k_rope, v):
    scale = (A_NOPE + A_ROPE) ** -0.5

    def blk(args):
        qn, qr = args
        s = (jnp.einsum('bqhd,bkhd->bhqk', qn, k_nope)
             + jnp.einsum('bqhr,bkr->bhqk', qr, k_rope)).astype(jnp.float32) * scale
        p = jax.nn.softmax(s, axis=-1)
        return jnp.einsum('bhqk,bkhd->bqhd', p.astype(v.dtype), v)

    return from_blocks(lax.map(blk, (to_blocks(q_nope), to_blocks(q_rope))))


def neighbourhood_attention(q, k, v, rpb):
    t_len = q.shape[1]
    rows = t_len // GRID_W
    kr = min(NA_ROWS, rows)
    ii = jnp.arange(kr)
    jj = jnp.arange(NA_COLS)
    nb = t_len // QBLK

    def blk(args):
        qb, b = args
        t = b * QBLK + jnp.arange(QBLK)
        r = t // GRID_W
        c = t % GRID_W
        rs = jnp.clip(r - kr // 2, 0, rows - kr)
        cs = jnp.clip(c - NA_COLS // 2, 0, GRID_W - NA_COLS)
        key_r = rs[:, None] + ii[None, :]
        key_c = cs[:, None] + jj[None, :]
        idx = (key_r[:, :, None] * GRID_W + key_c[:, None, :]).reshape(QBLK, kr * NA_COLS)
        dr = key_r - r[:, None] + (NA_ROWS - 1)
        dc = key_c - c[:, None] + (NA_COLS - 1)
        bias = rpb[:, dr[:, :, None], dc[:, None, :]].reshape(rpb.shape[0], QBLK, kr * NA_COLS)
        out, _ = gathered_attention(qb, k, v, idx, bias.astype(jnp.float32))
        return out

    return from_blocks(lax.map(blk, (to_blocks(q), jnp.arange(nb))))


def diff_attention(q1, q2, k1, k2, v, lam):
    scale = C_DIM ** -0.5

    def blk(args):
        q1b, q2b = args
        s1 = jnp.einsum('bqhd,bkhd->bhqk', q1b, k1).astype(jnp.float32) * scale
        s2 = jnp.einsum('bqhd,bkhd->bhqk', q2b, k2).astype(jnp.float32) * scale
        p = jax.nn.softmax(s1, axis=-1) - lam * jax.nn.softmax(s2, axis=-1)
        return jnp.einsum('bhqk,bkhv->bqhv', p.astype(v.dtype), v)

    return from_blocks(lax.map(blk, (to_blocks(q1), to_blocks(q2))))


def dilated_attention(q, k, v):
    t_len = q.shape[1]
    nb = t_len // QBLK

    def blk(args):
        qb, b = args
        t = b * QBLK + jnp.arange(QBLK)
        outs, lses = [], []
        for (win, dil) in D_BRANCHES:
            n_side = win // 2 // dil
            off = dil * jnp.arange(-n_side, n_side + 1)
            pos = t[:, None] + off[None, :]
            valid = (pos >= 0) & (pos < t_len)
            idx = jnp.clip(pos, 0, t_len - 1)
            mask = jnp.where(valid, 0.0, -jnp.inf).astype(jnp.float32)
            o, lse = gathered_attention(qb, k, v, idx, mask)
            outs.append(o)
            lses.append(lse)
        wts = jax.nn.softmax(jnp.stack(lses, axis=0), axis=0)
        out = outs[0] * jnp.swapaxes(wts[0], 1, 2)[..., None].astype(outs[0].dtype)
        for i in range(1, len(outs)):
            out = out + outs[i] * jnp.swapaxes(wts[i], 1, 2)[..., None].astype(outs[i].dtype)
        return out

    return from_blocks(lax.map(blk, (to_blocks(q), jnp.arange(nb))))


def token_mixers(h, lam_init, w_in, a_qnorm_g, a_w_uq, a_kvnorm_g, a_w_ukv,
                 b_rpb, c_lambda, c_subln_g, w_out):
    bsz, t_len, _ = h.shape
    pos = jnp.arange(t_len)
    proj = h @ w_in
    (a_cq, a_ckv, a_kr, b_q, b_k, b_v, c_q, c_k, c_v,
     d_q, d_k, d_v) = jnp.split(proj, split_points(), axis=-1)

    qa = (rmsnorm(a_cq, a_qnorm_g) @ a_w_uq).reshape(bsz, t_len, A_HEADS, A_NOPE + A_ROPE)
    kva = (rmsnorm(a_ckv, a_kvnorm_g) @ a_w_ukv).reshape(bsz, t_len, A_HEADS, A_NOPE + A_V)
    a_out = mla_attention(qa[..., :A_NOPE], rope(qa[..., A_NOPE:], pos),
                          kva[..., :A_NOPE], rope(a_kr, pos), kva[..., A_NOPE:])

    b_out = neighbourhood_attention(b_q.reshape(bsz, t_len, B_HEADS, B_DIM),
                                    b_k.reshape(bsz, t_len, B_HEADS, B_DIM),
                                    b_v.reshape(bsz, t_len, B_HEADS, B_DIM), b_rpb)

    cq = rope(c_q.reshape(bsz, t_len, C_HEADS, 2, C_DIM), pos)
    ck = rope(c_k.reshape(bsz, t_len, C_HEADS, 2, C_DIM), pos)
    lp = c_lambda.astype(jnp.float32)
    lam = jnp.exp(jnp.sum(lp[0] * lp[1])) - jnp.exp(jnp.sum(lp[2] * lp[3])) + lam_init
    c_o = diff_attention(cq[..., 0, :], cq[..., 1, :], ck[..., 0, :], ck[..., 1, :],
                         c_v.reshape(bsz, t_len, C_HEADS, C_VDIM), lam)
    c_out = rmsnorm(c_o, c_subln_g) * (1.0 - lam_init)

    d_out = dilated_attention(rope(d_q.reshape(bsz, t_len, D_HEADS, D_DIM), pos),
                              rope(d_k.reshape(bsz, t_len, D_HEADS, D_DIM), pos),
                              d_v.reshape(bsz, t_len, D_HEADS, D_DIM))

    mixed = jnp.concatenate([a_out.reshape(bsz, t_len, -1), b_out.reshape(bsz, t_len, -1),
                             c_out.reshape(bsz, t_len, -1), d_out.reshape(bsz, t_len, -1)], axis=-1)
    return mixed @ w_out


def expert_choice_ffn(h, w_router, w_gate, w_up, w_down):
    bsz, t_len, d = h.shape
    n_tok = bsz * t_len
    cap = CAP_FACTOR * n_tok // N_EXPERTS
    xf = h.reshape(n_tok, d)
    aff = jax.nn.softmax((xf @ w_router).astype(jnp.float32), axis=-1)
    gate, idx = lax.top_k(aff.T, cap)
    xg = xf[idx]
    hid = jax.nn.silu(jnp.einsum('ecd,edf->ecf', xg, w_gate)) * jnp.einsum('ecd,edf->ecf', xg, w_up)
    y = jnp.einsum('ecf,efd->ecd', hid, w_down) * gate[..., None].astype(h.dtype)
    out = jnp.zeros_like(xf).at[idx.reshape(-1)].add(y.reshape(-1, d))
    return out.reshape(bsz, t_len, d)


def trunk(x, norm1_g, w_in, a_qnorm_g, a_w_uq, a_kvnorm_g, a_w_ukv, b_rpb, c_lambda,
          c_subln_g, w_out, norm2_g, w_router, w_gate, w_up, w_down, final_g):
    for l in range(DEPTH):
        lam_init = 0.8 - 0.6 * math.exp(-0.3 * l)
        h = rmsnorm(x, norm1_g[l])
        x = x + token_mixers(h, lam_init, w_in[l], a_qnorm_g[l], a_w_uq[l], a_kvnorm_g[l],
                             a_w_ukv[l], b_rpb[l], c_lambda[l], c_subln_g[l], w_out[l])
        h = rmsnorm(x, norm2_g[l])
        x = x + expert_choice_ffn(h, w_router[l], w_gate[l], w_up[l], w_down[l])
    return rmsnorm(x, final_g)


def setup_inputs(seed: int = 0) -> dict:
    key = jax.random.key(seed)
    ks = jax.random.split(key, 20)
    nrm = jax.random.normal
    f32 = jnp.float32
    return {
        "x_prompt": nrm(ks[0], (BATCH, SEQ, D_MODEL), f32),
        "x_sample": nrm(ks[1], (DEC_BATCH, DEC_SEQ, D_MODEL), f32),
        "norm1_g": 1.0 + 0.02 * nrm(ks[2], (DEPTH, D_MODEL), f32),
        "w_in": nrm(ks[3], (DEPTH, D_MODEL, IN_W), f32) * D_MODEL ** -0.5,
        "a_qnorm_g": 1.0 + 0.02 * nrm(ks[4], (DEPTH, A_Q_LORA), f32),
        "a_w_uq": nrm(ks[5], (DEPTH, A_Q_LORA, A_HEADS * (A_NOPE + A_ROPE)), f32) * A_Q_LORA ** -0.5,
        "a_kvnorm_g": 1.0 + 0.02 * nrm(ks[6], (DEPTH, A_KV_LORA), f32),
        "a_w_ukv": nrm(ks[7], (DEPTH, A_KV_LORA, A_HEADS * (A_NOPE + A_V)), f32) * A_KV_LORA ** -0.5,
        "b_rpb": 0.02 * nrm(ks[8], (DEPTH, B_HEADS, 2 * NA_ROWS - 1, 2 * NA_COLS - 1), f32),
        "c_lambda": 0.1 * nrm(ks[9], (DEPTH, 4, C_DIM), f32),
        "c_subln_g": 1.0 + 0.02 * nrm(ks[10], (DEPTH, C_VDIM), f32),
        "w_out": nrm(ks[11], (DEPTH, MIX_W, D_MODEL), f32) * MIX_W ** -0.5,
        "norm2_g": 1.0 + 0.02 * nrm(ks[12], (DEPTH, D_MODEL), f32),
        "w_router": nrm(ks[13], (DEPTH, D_MODEL, N_EXPERTS), f32) * D_MODEL ** -0.5,
        "w_gate": nrm(ks[14], (DEPTH, N_EXPERTS, D_MODEL, EXPERT_FF), f32) * D_MODEL ** -0.5,
        "w_up": nrm(ks[15], (DEPTH, N_EXPERTS, D_MODEL, EXPERT_FF), f32) * D_MODEL ** -0.5,
        "w_down": nrm(ks[16], (DEPTH, N_EXPERTS, EXPERT_FF, D_MODEL), f32) * EXPERT_FF ** -0.5,
        "final_g": 1.0 + 0.02 * nrm(ks[17], (D_MODEL,), f32),
    }


def reference(x_prompt, x_sample, norm1_g, w_in, a_qnorm_g, a_w_uq, a_kvnorm_g, a_w_ukv,
              b_rpb, c_lambda, c_subln_g, w_out, norm2_g, w_router, w_gate, w_up, w_down, final_g):
    y_prompt = trunk(x_prompt, norm1_g, w_in, a_qnorm_g, a_w_uq, a_kvnorm_g, a_w_ukv, b_rpb,
                     c_lambda, c_subln_g, w_out, norm2_g, w_router, w_gate, w_up, w_down, final_g)
    y_sample = trunk(x_sample, norm1_g, w_in, a_qnorm_g, a_w_uq, a_kvnorm_g, a_w_ukv, b_rpb,
                     c_lambda, c_subln_g, w_out, norm2_g, w_router, w_gate, w_up, w_down, final_g)
    return (y_prompt, y_sample)
```

```python
import functools
import math

import numpy as np
import jax
import jax.numpy as jnp
from jax import lax
from jax.experimental import pallas as pl
from jax.experimental.pallas import tpu as pltpu

F32 = jnp.float32
BF16 = jnp.bfloat16

D_MODEL = 1024
N_HEADS = 4
HEAD_DIM = 64
MIX_COLS = N_HEADS * HEAD_DIM
A_NOPE = 64
A_ROPE = 32
A_PAD = 128
A_Q_LORA = 256
A_KV_LORA = 128
C_DIM = 32
GRID_W = 64
NA_ROWS = 8
NA_COLS = 16
QBLK = 128
B_WIN = 5 * QBLK
D_REACH = 1024
N_EXPERTS = 16
CAP_FACTOR = 2
ROPE_THETA = 10000.0
RMS_EPS = 1e-6
NEG_BIG = -1e30
LANES = 128
V7X_VMEM_LIMIT = 56 * 1024 * 1024

PROJ_SIZES = (256, 128, 32, 256, 256, 256, 256, 256, 256, 256, 256, 256)

_M_ACQ, _M_ACKV, _M_BQ, _M_BK, _M_BV = 0, 256, 384, 640, 896
_M_CQ, _M_CQR, _M_CV = 1152, 1408, 1664
_M_DQ, _M_DQR, _M_DK, _M_DKR, _M_DV = 1920, 2176, 2432, 2688, 2944
_M_COLS = 3200
_T_CK, _T_CKR, _T_AKR, _T_AKRR, _T_ROWS = 0, 256, 512, 544, 576


def _cparams(sem):
    return pltpu.CompilerParams(dimension_semantics=sem, vmem_limit_bytes=V7X_VMEM_LIMIT)


def _rms(x):
    return x * lax.rsqrt(jnp.mean(x * x, axis=-1, keepdims=True) + RMS_EPS)


def _nt_dot(a, b):
    return lax.dot_general(a, b, (((1,), (1,)), ((), ())), preferred_element_type=F32)


def _head_of_lane(shape):
    return lax.broadcasted_iota(jnp.int32, shape, len(shape) - 1) // HEAD_DIM


def _proj_kernel(x_ref, g1_ref, wm_ref, wt_ref, gq_ref, wuq_ref, gkv_ref, wukt_ref, wuv_ref,
                 tab_ref, tabt_ref,
                 qa_ref, kat_ref, va_ref, qb_ref, kb_ref, vb_ref, qc_ref, kct_ref, vc_ref,
                 qd_ref, kd_ref, vd_ref):
    x = x_ref[...]
    h = (_rms(x) * g1_ref[...]).astype(BF16)
    p = jnp.dot(h, wm_ref[...], preferred_element_type=F32)
    pt = _nt_dot(wt_ref[...], h)
    tab = tab_ref[...]
    cos_a, sin_a, cos_c, sin_c, cos_d, sin_d = (tab[:, LANES * i:LANES * (i + 1)] for i in range(6))
    tabt = tabt_ref[...]
    cos_t, sin_t = tabt[0:A_ROPE], tabt[A_ROPE:2 * A_ROPE]

    scale_a = (A_NOPE + A_ROPE) ** -0.5
    latq = (_rms(p[:, _M_ACQ:_M_ACQ + A_Q_LORA]) * gq_ref[...]).astype(BF16)
    qa2 = jnp.dot(latq, wuq_ref[...], preferred_element_type=F32)
    for hd in range(N_HEADS):
        lo = A_PAD * hd
        blk = qa2[:, lo:lo + A_PAD] * cos_a + qa2[:, N_HEADS * A_PAD + lo:N_HEADS * A_PAD + lo + A_PAD] * sin_a
        qa_ref[:, lo:lo + A_PAD] = (blk * scale_a).astype(BF16)
    latkv = (_rms(p[:, _M_ACKV:_M_ACKV + A_KV_LORA]) * gkv_ref[...]).astype(BF16)
    va_ref[...] = jnp.dot(latkv, wuv_ref[...], preferred_element_type=F32).astype(BF16)
    knt = _nt_dot(wukt_ref[...], latkv)
    krt = (pt[_T_AKR:_T_AKR + A_ROPE] * cos_t + pt[_T_AKRR:_T_AKRR + A_ROPE] * sin_t).astype(BF16)
    tm = x.shape[0]
    for hd in range(N_HEADS):
        lo = A_PAD * hd
        kat_ref[lo:lo + A_NOPE, :] = knt[A_NOPE * hd:A_NOPE * (hd + 1)].astype(BF16)
        kat_ref[lo + A_NOPE:lo + A_NOPE + A_ROPE, :] = krt
        kat_ref[lo + A_NOPE + A_ROPE:lo + A_PAD, :] = jnp.zeros((A_PAD - A_NOPE - A_ROPE, tm), BF16)

    qb_ref[...] = (p[:, _M_BQ:_M_BQ + MIX_COLS] * (HEAD_DIM ** -0.5)).astype(BF16)
    kb_ref[...] = p[:, _M_BK:_M_BK + MIX_COLS].astype(BF16)
    vb_ref[...] = p[:, _M_BV:_M_BV + MIX_COLS].astype(BF16)

    scale_c = C_DIM ** -0.5
    for j in range(MIX_COLS // LANES):
        lo = LANES * j
        blk = p[:, _M_CQ + lo:_M_CQ + lo + LANES] * cos_c + p[:, _M_CQR + lo:_M_CQR + lo + LANES] * sin_c
        qc_ref[:, lo:lo + LANES] = (blk * scale_c).astype(BF16)
    reps = MIX_COLS // A_ROPE
    cos_ct = jnp.concatenate([cos_t] * reps, axis=0)
    sin_ct = jnp.concatenate([sin_t] * reps, axis=0)
    kct_ref[...] = (pt[_T_CK:_T_CK + MIX_COLS] * cos_ct + pt[_T_CKR:_T_CKR + MIX_COLS] * sin_ct).astype(BF16)
    vc_ref[...] = p[:, _M_CV:_M_CV + MIX_COLS].astype(BF16)

    for j in range(MIX_COLS // LANES):
        lo = LANES * j
        qblk = p[:, _M_DQ + lo:_M_DQ + lo + LANES] * cos_d + p[:, _M_DQR + lo:_M_DQR + lo + LANES] * sin_d
        qd_ref[:, lo:lo + LANES] = (qblk * (HEAD_DIM ** -0.5)).astype(BF16)
        kblk = p[:, _M_DK + lo:_M_DK + lo + LANES] * cos_d + p[:, _M_DKR + lo:_M_DKR + lo + LANES] * sin_d
        kd_ref[:, lo:lo + LANES] = kblk.astype(BF16)
    vd_ref[...] = p[:, _M_DV:_M_DV + MIX_COLS].astype(BF16)


def _proj_call(x2d, t_len, g1, wm, wt, gq, wuq, gkv, wukt, wuv, tab, tabt, tm):
    n = x2d.shape[0]
    nt = t_len // tm
    full = lambda a: pl.BlockSpec(a.shape, lambda i: (0,) * a.ndim)
    row = lambda w: pl.BlockSpec((tm, w), lambda i: (i, 0))
    col = lambda r: pl.BlockSpec((r, tm), lambda i: (0, i))
    tok = lambda w: jax.ShapeDtypeStruct((n, w), BF16)
    out_shape = (tok(N_HEADS * A_PAD), jax.ShapeDtypeStruct((N_HEADS * A_PAD, n), BF16), tok(MIX_COLS),
                 tok(MIX_COLS), tok(MIX_COLS), tok(MIX_COLS),
                 tok(MIX_COLS), jax.ShapeDtypeStruct((MIX_COLS, n), BF16), tok(MIX_COLS),
                 tok(MIX_COLS), tok(MIX_COLS), tok(MIX_COLS))
    out_specs = (row(N_HEADS * A_PAD), col(N_HEADS * A_PAD), row(MIX_COLS),
                 row(MIX_COLS), row(MIX_COLS), row(MIX_COLS),
                 row(MIX_COLS), col(MIX_COLS), row(MIX_COLS),
                 row(MIX_COLS), row(MIX_COLS), row(MIX_COLS))
    in_specs = [row(D_MODEL), full(g1), full(wm), full(wt), full(gq), full(wuq), full(gkv), full(wukt), full(wuv),
                pl.BlockSpec((tm, tab.shape[1]), lambda i: (i % nt, 0)),
                pl.BlockSpec((tabt.shape[0], tm), lambda i: (0, i % nt))]
    return pl.pallas_call(
        _proj_kernel, grid=(n // tm,), in_specs=in_specs, out_specs=out_specs, out_shape=out_shape,
        compiler_params=_cparams(("parallel",)), name="proj_in",
    )(x2d, g1, wm, wt, gq, wuq, gkv, wukt, wuv, tab, tabt)


def _online_step(s, v, m_ref, l_ref, acc_ref, idx):
    m_prev = m_ref[idx]
    m_new = jnp.maximum(m_prev, jnp.max(s, axis=1, keepdims=True))
    alpha = jnp.exp(m_prev - m_new)
    p = jnp.exp(s - m_new)
    l_ref[idx] = alpha * l_ref[idx] + jnp.sum(p, axis=1, keepdims=True)
    acc_ref[idx] = alpha * acc_ref[idx] + jnp.dot(p.astype(BF16), v, preferred_element_type=F32)
    m_ref[idx] = m_new


def _mla_kernel(q_ref, kt_ref, v_ref, o_ref, m_ref, l_ref, acc_ref):
    j = pl.program_id(2)

    @pl.when(j == 0)
    def _():
        m_ref[...] = jnp.full(m_ref.shape, -jnp.inf, F32)
        l_ref[...] = jnp.zeros(l_ref.shape, F32)
        acc_ref[...] = jnp.zeros(acc_ref.shape, F32)

    v = v_ref[...]
    for hd in range(N_HEADS):
        lo = A_PAD * hd
        s = jnp.dot(q_ref[:, lo:lo + A_PAD], kt_ref[lo:lo + A_PAD, :], preferred_element_type=F32)
        _online_step(s, v, m_ref, l_ref, acc_ref, hd)

    @pl.when(j == pl.num_programs(2) - 1)
    def _():
        head = _head_of_lane(o_ref.shape)
        out = jnp.zeros(o_ref.shape, F32)
        for hd in range(N_HEADS):
            out = out + jnp.where(head == hd, acc_ref[hd] * (1.0 / l_ref[hd]), 0.0)
        o_ref[...] = out.astype(o_ref.dtype)


def _mla_call(q, kt, v, bsz, t_len, tq, tk):
    n = q.shape[0]
    nq, nk = t_len // tq, t_len // tk
    return pl.pallas_call(
        _mla_kernel, grid=(bsz, nq, nk),
        in_specs=[pl.BlockSpec((tq, N_HEADS * A_PAD), lambda b, i, j: (b * nq + i, 0)),
                  pl.BlockSpec((N_HEADS * A_PAD, tk), lambda b, i, j: (0, b * nk + j)),
                  pl.BlockSpec((tk, MIX_COLS), lambda b, i, j: (b * nk + j, 0))],
        out_specs=pl.BlockSpec((tq, MIX_COLS), lambda b, i, j: (b * nq + i, 0)),
        out_shape=jax.ShapeDtypeStruct((n, MIX_COLS), BF16),
        scratch_shapes=[pltpu.VMEM((N_HEADS, tq, 1), F32), pltpu.VMEM((N_HEADS, tq, 1), F32),
                        pltpu.VMEM((N_HEADS, tq, MIX_COLS), F32)],
        compiler_params=_cparams(("parallel", "parallel", "arbitrary")), name="attn_mla",
    )(q, kt, v)


def _diff_kernel(lam_init, q_ref, kt_ref, v_ref, lam_ref, g_ref, o_ref, qm_ref, m_ref, l_ref, acc_ref):
    j = pl.program_id(2)
    n_maps = 2 * N_HEADS

    @pl.when(j == 0)
    def _():
        m_ref[...] = jnp.full(m_ref.shape, -jnp.inf, F32)
        l_ref[...] = jnp.zeros(l_ref.shape, F32)
        acc_ref[...] = jnp.zeros(acc_ref.shape, F32)
        q = q_ref[...]
        group = lax.broadcasted_iota(jnp.int32, q.shape, 1) // C_DIM
        for mi in range(n_maps):
            qm_ref[mi] = jnp.where(group == mi, q, jnp.zeros_like(q))

    v = v_ref[...]
    kt = kt_ref[...]
    for mi in range(n_maps):
        s = jnp.dot(qm_ref[mi], kt, preferred_element_type=F32)
        _online_step(s, v, m_ref, l_ref, acc_ref, mi)

    @pl.when(j == pl.num_programs(2) - 1)
    def _():
        lam = lam_ref[...]
        head = _head_of_lane(o_ref.shape)
        o = jnp.zeros(o_ref.shape, F32)
        for hd in range(N_HEADS):
            oh = acc_ref[2 * hd] * (1.0 / l_ref[2 * hd]) - lam * (acc_ref[2 * hd + 1] * (1.0 / l_ref[2 * hd + 1]))
            o = o + jnp.where(head == hd, oh, 0.0)
        o2 = o * o
        inv = jnp.zeros(o_ref.shape, F32)
        for hd in range(N_HEADS):
            ms = jnp.sum(jnp.where(head == hd, o2, 0.0), axis=1, keepdims=True) * (1.0 / HEAD_DIM)
            inv = inv + jnp.where(head == hd, lax.rsqrt(ms + RMS_EPS), 0.0)
        o_ref[...] = ((o * inv * g_ref[...]) * (1.0 - lam_init)).astype(o_ref.dtype)


def _diff_call(q, kt, v, lam, g_tiled, lam_init, bsz, t_len, tq, tk):
    n = q.shape[0]
    nq, nk = t_len // tq, t_len // tk
    n_maps = 2 * N_HEADS
    return pl.pallas_call(
        functools.partial(_diff_kernel, lam_init), grid=(bsz, nq, nk),
        in_specs=[pl.BlockSpec((tq, MIX_COLS), lambda b, i, j: (b * nq + i, 0)),
                  pl.BlockSpec((MIX_COLS, tk), lambda b, i, j: (0, b * nk + j)),
                  pl.BlockSpec((tk, MIX_COLS), lambda b, i, j: (b * nk + j, 0)),
                  pl.BlockSpec((1, 1), lambda b, i, j: (0, 0)),
                  pl.BlockSpec((1, MIX_COLS), lambda b, i, j: (0, 0))],
        out_specs=pl.BlockSpec((tq, MIX_COLS), lambda b, i, j: (b * nq + i, 0)),
        out_shape=jax.ShapeDtypeStruct((n, MIX_COLS), BF16),
        scratch_shapes=[pltpu.VMEM((n_maps, tq, MIX_COLS), BF16),
                        pltpu.VMEM((n_maps, tq, 1), F32), pltpu.VMEM((n_maps, tq, 1), F32),
                        pltpu.VMEM((n_maps, tq, MIX_COLS), F32)],
        compiler_params=_cparams(("parallel", "parallel", "arbitrary")), name="attn_diff",
    )(q, kt, v, lam, g_tiled)


def _window_heads(q, kw, vw, add, mult, o_ref):
    head = _head_of_lane(q.shape)
    out = jnp.zeros(q.shape, F32)
    for hd in range(N_HEADS):
        qh = jnp.where(head == hd, q, jnp.zeros_like(q))
        s = _nt_dot(qh, kw)
        if add is not None:
            s = s + add(hd)
        if mult is not None:
            s = jnp.where(mult > 0.0, s, NEG_BIG)
        m = jnp.max(s, axis=1, keepdims=True)
        p = jnp.exp(s - m)
        if mult is not None:
            p = p * mult
        l = jnp.sum(p, axis=1, keepdims=True)
        o = jnp.dot(p.astype(BF16), vw, preferred_element_type=F32)
        out = out + jnp.where(head == hd, o * (1.0 / l), 0.0)
    o_ref[...] = out.astype(o_ref.dtype)


def _nbr_kernel(nb, q_ref, k_ref, v_ref, bias_ref, o_ref):
    i = pl.program_id(1)
    start = pl.multiple_of(jnp.clip(i - 2, 0, nb - B_WIN // QBLK) * QBLK, QBLK)
    kw = k_ref[pl.ds(start, B_WIN), :]
    vw = v_ref[pl.ds(start, B_WIN), :]
    _window_heads(q_ref[...], kw, vw, lambda hd: bias_ref[0, hd], None, o_ref)


def _nbr_block_type(i, nb):
    return jnp.where(i < 2, i, jnp.where(i > nb - 3, i - nb + 5, 2))


def _nbr_call(q, k, v, bias, bsz, t_len):
    n = q.shape[0]
    nb = t_len // QBLK
    return pl.pallas_call(
        functools.partial(_nbr_kernel, nb), grid=(bsz, nb),
        in_specs=[pl.BlockSpec((QBLK, MIX_COLS), lambda b, i: (b * nb + i, 0)),
                  pl.BlockSpec((t_len, MIX_COLS), lambda b, i: (b, 0)),
                  pl.BlockSpec((t_len, MIX_COLS), lambda b, i: (b, 0)),
                  pl.BlockSpec((1, N_HEADS, QBLK, B_WIN), lambda b, i: (_nbr_block_type(i, nb), 0, 0, 0))],
        out_specs=pl.BlockSpec((QBLK, MIX_COLS), lambda b, i: (b * nb + i, 0)),
        out_shape=jax.ShapeDtypeStruct((n, MIX_COLS), BF16),
        compiler_params=_cparams(("parallel", "arbitrary")), name="attn_nbr",
    )(q, k, v, bias)


def _nbr_bias_tables(rpb, t_len):
    rows = t_len // GRID_W
    nb = t_len // QBLK
    kr = min(NA_ROWS, rows)
    reps = (0, 1, 2, nb - 2, nb - 1)
    tables = []
    for b in reps:
        t = b * QBLK + np.arange(QBLK)
        r, c = t // GRID_W, t % GRID_W
        rs = np.clip(r - kr // 2, 0, rows - kr)
        cs = np.clip(c - NA_COLS // 2, 0, GRID_W - NA_COLS)
        key_r = rs[:, None] + np.arange(kr)[None, :]
        key_c = cs[:, None] + np.arange(NA_COLS)[None, :]
        idx = (key_r[:, :, None] * GRID_W + key_c[:, None, :]).reshape(QBLK, kr * NA_COLS)
        dr = key_r - r[:, None] + (NA_ROWS - 1)
        dc = key_c - c[:, None] + (NA_COLS - 1)
        vals = rpb[:, dr[:, :, None], dc[:, None, :]].reshape(N_HEADS, QBLK, kr * NA_COLS).astype(F32)
        start = int(np.clip(b - 2, 0, nb - B_WIN // QBLK)) * QBLK
        cols = idx - start
        dense = jnp.full((N_HEADS, QBLK, B_WIN), NEG_BIG, F32)
        dense = dense.at[:, np.arange(QBLK)[:, None], cols].set(vals)
        tables.append(dense)
    return jnp.stack(tables, axis=0)


def _dil_kernel(t_len, tq, q_ref, k_ref, v_ref, o_ref):
    i = pl.program_id(1)
    w = tq + 2 * D_REACH
    t0 = i * tq
    start = pl.multiple_of(jnp.clip(t0 - D_REACH, 0, t_len - w), tq)
    kw = k_ref[pl.ds(start, w), :]
    vw = v_ref[pl.ds(start, w), :]
    d = (start - t0) + lax.broadcasted_iota(jnp.int32, (tq, w), 1) - lax.broadcasted_iota(jnp.int32, (tq, w), 0)
    ad = jnp.abs(d)
    mult = ((ad <= 64).astype(F32)
            + (((d & 3) == 0) & (ad <= 256)).astype(F32)
            + (((d & 15) == 0) & (ad <= D_REACH)).astype(F32))
    _window_heads(q_ref[...], kw, vw, None, mult, o_ref)


def _dil_call(q, k, v, bsz, t_len, tq):
    n = q.shape[0]
    nq = t_len // tq
    return pl.pallas_call(
        functools.partial(_dil_kernel, t_len, tq), grid=(bsz, nq),
        in_specs=[pl.BlockSpec((tq, MIX_COLS), lambda b, i: (b * nq + i, 0)),
                  pl.BlockSpec((t_len, MIX_COLS), lambda b, i: (b, 0)),
                  pl.BlockSpec((t_len, MIX_COLS), lambda b, i: (b, 0))],
        out_specs=pl.BlockSpec((tq, MIX_COLS), lambda b, i: (b * nq + i, 0)),
        out_shape=jax.ShapeDtypeStruct((n, MIX_COLS), BF16),
        compiler_params=_cparams(("parallel", "arbitrary")), name="attn_dil",
    )(q, k, v)


def _out_kernel(oa_ref, ob_ref, oc_ref, od_ref, x_ref, wo_ref, g2_ref, wr_ref, x2_ref, hn_ref, aff_ref):
    acc = x_ref[...]
    for mi, o_ref in enumerate((oa_ref, ob_ref, oc_ref, od_ref)):
        acc = acc + jnp.dot(o_ref[...], wo_ref[MIX_COLS * mi:MIX_COLS * (mi + 1), :], preferred_element_type=F32)
    x2_ref[...] = acc
    hn = _rms(acc) * g2_ref[...]
    hn_ref[...] = hn
    logits = jnp.dot(hn, wr_ref[...], preferred_element_type=F32, precision=lax.Precision.HIGHEST)
    lane = lax.broadcasted_iota(jnp.int32, logits.shape, 1)
    logits = jnp.where(lane < N_EXPERTS, logits, -jnp.inf)
    m = jnp.max(logits, axis=1, keepdims=True)
    e = jnp.exp(logits - m)
    aff_ref[...] = e / jnp.sum(e, axis=1, keepdims=True)


def _out_call(oa, ob, oc, od, x2d, wo, g2, wr, tm):
    n = x2d.shape[0]
    full = lambda a: pl.BlockSpec(a.shape, lambda i: (0,) * a.ndim)
    row = lambda w: pl.BlockSpec((tm, w), lambda i: (i, 0))
    return pl.pallas_call(
        _out_kernel, grid=(n // tm,),
        in_specs=[row(MIX_COLS)] * 4 + [row(D_MODEL), full(wo), full(g2), full(wr)],
        out_specs=(row(D_MODEL), row(D_MODEL), row(LANES)),
        out_shape=(jax.ShapeDtypeStruct((n, D_MODEL), F32), jax.ShapeDtypeStruct((n, D_MODEL), F32),
                   jax.ShapeDtypeStruct((n, LANES), F32)),
        compiler_params=_cparams(("parallel",)), name="proj_out",
    )(oa, ob, oc, od, x2d, wo, g2, wr)


def _ffn_kernel(ts, idx_ref, gate_ref, hn_hbm, wg_ref, wu_ref, wd_ref, y_ref, xbuf, wgb, wub, wdb, sem):
    i = pl.program_id(1)

    @pl.when(i == 0)
    def _():
        wgb[...] = wg_ref[0, 0].astype(BF16)
        wub[...] = wu_ref[0, 0].astype(BF16)
        wdb[...] = wd_ref[0, 0].astype(BF16)

    def row_copy(s):
        return pltpu.make_async_copy(hn_hbm.at[pl.ds(idx_ref[0, 0, s], 1), :], xbuf.at[pl.ds(s, 1), :], sem)

    def start(s, c):
        row_copy(s).start()
        return c

    def wait(s, c):
        row_copy(s).wait()
        return c

    lax.fori_loop(0, ts, start, 0)
    lax.fori_loop(0, ts, wait, 0)
    xg = xbuf[...].astype(BF16)
    a = jnp.dot(xg, wgb[...], preferred_element_type=F32)
    b = jnp.dot(xg, wub[...], preferred_element_type=F32)
    hid = (a * jax.nn.sigmoid(a) * b).astype(BF16)
    y_ref[...] = jnp.dot(hid, wdb[...], preferred_element_type=F32) * gate_ref[...]


def _ffn_call(idx, gate, hn, w_gate, w_up, w_down, layer, ts):
    n_exp, cap = idx.shape
    nt = cap // ts
    idx3 = idx.reshape(n_exp * nt, 1, ts)
    gate2 = gate.reshape(n_exp * cap, 1)
    wspec = pl.BlockSpec((1, 1, D_MODEL, D_MODEL), lambda e, i: (layer, e, 0, 0))
    return pl.pallas_call(
        functools.partial(_ffn_kernel, ts), grid=(n_exp, nt),
        in_specs=[pl.BlockSpec((1, 1, ts), lambda e, i: (e * nt + i, 0, 0), memory_space=pltpu.SMEM),
                  pl.BlockSpec((ts, 1), lambda e, i: (e * nt + i, 0)),
                  pl.BlockSpec(memory_space=pl.ANY),
                  wspec, wspec, wspec],
        out_specs=pl.BlockSpec((ts, D_MODEL), lambda e, i: (e * nt + i, 0)),
        out_shape=jax.ShapeDtypeStruct((n_exp * cap, D_MODEL), F32),
        scratch_shapes=[pltpu.VMEM((ts, D_MODEL), F32),
                        pltpu.VMEM((D_MODEL, D_MODEL), BF16), pltpu.VMEM((D_MODEL, D_MODEL), BF16),
                        pltpu.VMEM((D_MODEL, D_MODEL), BF16), pltpu.SemaphoreType.DMA(())],
        compiler_params=_cparams(("arbitrary", "arbitrary")), name="expert_ffn",
    )(idx3, gate2, hn, w_gate, w_up, w_down)


def _final_kernel(x_ref, g_ref, o_ref):
    o_ref[...] = _rms(x_ref[...]) * g_ref[...]


def _final_call(x2d, g, tm):
    n = x2d.shape[0]
    return pl.pallas_call(
        _final_kernel, grid=(n // tm,),
        in_specs=[pl.BlockSpec((tm, D_MODEL), lambda i: (i, 0)), pl.BlockSpec((1, D_MODEL), lambda i: (0, 0))],
        out_specs=pl.BlockSpec((tm, D_MODEL), lambda i: (i, 0)),
        out_shape=jax.ShapeDtypeStruct((n, D_MODEL), F32),
        compiler_params=_cparams(("parallel",)), name="final_norm",
    )(x2d, g)


def _rot_cols(w, d):
    k, c = w.shape
    half = d // 2
    wg = w.reshape(k, c // d, 2, half)
    return jnp.concatenate([-wg[:, :, 1], wg[:, :, 0]], axis=2).reshape(k, c)


def _rope_tables(t_len):
    pos = jnp.arange(t_len, dtype=F32)

    def cs(d):
        half = d // 2
        inv = ROPE_THETA ** (-jnp.arange(half, dtype=F32) / half)
        ang = pos[:, None] * inv[None, :]
        return (jnp.concatenate([jnp.cos(ang)] * 2, axis=1), jnp.concatenate([jnp.sin(ang)] * 2, axis=1))

    c32, s32 = cs(A_ROPE)
    c64, s64 = cs(HEAD_DIM)
    ones = jnp.ones((t_len, A_NOPE), F32)
    zeros = jnp.zeros((t_len, A_NOPE), F32)
    pad = jnp.zeros((t_len, A_PAD - A_NOPE - A_ROPE), F32)
    tab = jnp.concatenate([ones, c32, pad, zeros, s32, pad,
                           jnp.tile(c32, (1, LANES // A_ROPE)), jnp.tile(s32, (1, LANES // A_ROPE)),
                           jnp.tile(c64, (1, LANES // HEAD_DIM)), jnp.tile(s64, (1, LANES // HEAD_DIM))], axis=1)
    tabt = jnp.concatenate([c32.T, s32.T], axis=0)
    return tab, tabt


def _layer_weights(l, w_in, a_w_uq, a_w_ukv, w_out, w_router):
    pts = np.cumsum(PROJ_SIZES)[:-1]
    (a_cq, a_ckv, a_kr, b_q, b_k, b_v, c_q, c_k, c_v, d_q, d_k, d_v) = jnp.split(w_in[l], pts, axis=1)
    wm = jnp.concatenate([a_cq, a_ckv, b_q, b_k, b_v, c_q, _rot_cols(c_q, C_DIM), c_v,
                          d_q, _rot_cols(d_q, HEAD_DIM), d_k, _rot_cols(d_k, HEAD_DIM), d_v], axis=1).astype(BF16)
    wt = jnp.concatenate([c_k, _rot_cols(c_k, C_DIM), a_kr, _rot_cols(a_kr, A_ROPE)], axis=1).T.astype(BF16)
    uq = a_w_uq[l].reshape(A_Q_LORA, N_HEADS, A_NOPE + A_ROPE)
    zpad = jnp.zeros((A_Q_LORA, N_HEADS, A_PAD - A_NOPE - A_ROPE), F32)
    uq_rope = uq[:, :, A_NOPE:]
    uq_rot = _rot_cols(uq_rope.reshape(A_Q_LORA, N_HEADS * A_ROPE), A_ROPE).reshape(A_Q_LORA, N_HEADS, A_ROPE)
    plain = jnp.concatenate([uq, zpad], axis=2).reshape(A_Q_LORA, N_HEADS * A_PAD)
    rot = jnp.concatenate([jnp.zeros_like(uq[:, :, :A_NOPE]), uq_rot, zpad], axis=2).reshape(A_Q_LORA, N_HEADS * A_PAD)
    wuq = jnp.concatenate([plain, rot], axis=1).astype(BF16)
    ukv = a_w_ukv[l].reshape(A_KV_LORA, N_HEADS, 2 * HEAD_DIM)
    wukt = ukv[:, :, :A_NOPE].reshape(A_KV_LORA, MIX_COLS).T.astype(BF16)
    wuv = ukv[:, :, A_NOPE:].reshape(A_KV_LORA, MIX_COLS).astype(BF16)
    wo = w_out[l].astype(BF16)
    wr = jnp.concatenate([w_router[l], jnp.zeros((D_MODEL, LANES - N_EXPERTS), F32)], axis=1)
    return wm, wt, wuq, wukt, wuv, wo, wr


def _trunk(x, norm1_g, w_in, a_qnorm_g, a_w_uq, a_kvnorm_g, a_w_ukv, b_rpb, c_lambda, c_subln_g, w_out,
           norm2_g, w_router, w_gate, w_up, w_down, final_g):
    bsz, t_len, _ = x.shape
    n = bsz * t_len
    tm = 512
    tq, tk = 512, 1024
    tq_dil = 256
    ts = 256
    depth = w_in.shape[0]
    cap = CAP_FACTOR * n // N_EXPERTS
    assert t_len % tk == 0 and t_len >= tq_dil + 2 * D_REACH and t_len // QBLK >= 5 and cap % ts == 0
    tab, tabt = _rope_tables(t_len)
    x2d = x.reshape(n, D_MODEL)
    for l in range(depth):
        lam_init = 0.8 - 0.6 * math.exp(-0.3 * l)
        wm, wt, wuq, wukt, wuv, wo, wr = _layer_weights(l, w_in, a_w_uq, a_w_ukv, w_out, w_router)
        row = lambda g: g[l].reshape(1, -1)
        (qa, kat, va, qb, kb, vb, qc, kct, vc, qd, kd, vd) = _proj_call(
            x2d, t_len, row(norm1_g), wm, wt, row(a_qnorm_g), wuq, row(a_kvnorm_g), wukt, wuv, tab, tabt, tm)
        oa = _mla_call(qa, kat, va, bsz, t_len, tq, tk)
        ob = _nbr_call(qb, kb, vb, _nbr_bias_tables(b_rpb[l], t_len), bsz, t_len)
        lp = c_lambda[l].astype(F32)
        lam = (jnp.exp(jnp.sum(lp[0] * lp[1])) - jnp.exp(jnp.sum(lp[2] * lp[3])) + lam_init).reshape(1, 1)
        g_sub = jnp.tile(c_subln_g[l], N_HEADS).reshape(1, MIX_COLS)
        oc = _diff_call(qc, kct, vc, lam, g_sub, lam_init, bsz, t_len, tq, tk)
        od = _dil_call(qd, kd, vd, bsz, t_len, tq_dil)
        x2, hn, aff = _out_call(oa, ob, oc, od, x2d, wo, row(norm2_g), wr, tm)
        gate, idx = lax.top_k(aff[:, :N_EXPERTS].T, cap)
        y = _ffn_call(idx, gate, hn, w_gate, w_up, w_down, l, ts)
        x2d = x2.at[idx.reshape(-1)].add(y)
    return _final_call(x2d, final_g.reshape(1, D_MODEL), tm).reshape(bsz, t_len, D_MODEL)


def kernel(x_prompt, x_sample, norm1_g, w_in, a_qnorm_g, a_w_uq, a_kvnorm_g, a_w_ukv, b_rpb, c_lambda, c_subln_g,
           w_out, norm2_g, w_router, w_gate, w_up, w_down, final_g):
    params = (norm1_g, w_in, a_qnorm_g, a_w_uq, a_kvnorm_g, a_w_ukv, b_rpb, c_lambda, c_subln_g, w_out,
              norm2_g, w_router, w_gate, w_up, w_down, final_g)
    return (_trunk(x_prompt, *params), _trunk(x_sample, *params))
```

```python
import functools
import math

import numpy as np
import jax
import jax.numpy as jnp
from jax import lax
from jax.experimental import pallas as pl
from jax.experimental.pallas import tpu as pltpu

F32 = jnp.float32
BF16 = jnp.bfloat16

D_MODEL = 1024
N_HEADS = 4
HEAD_DIM = 64
MIX_COLS = N_HEADS * HEAD_DIM
A_NOPE = 64
A_ROPE = 32
A_PAD = 128
A_Q_LORA = 256
A_KV_LORA = 128
C_DIM = 32
GRID_W = 64
NA_ROWS = 8
NA_COLS = 16
QBLK = 128
B_WIN = 5 * QBLK
D_REACH = 1024
N_EXPERTS = 16
CAP_FACTOR = 2
ROPE_THETA = 10000.0
RMS_EPS = 1e-6
NEG_BIG = -1e30
LOG2E = math.log2(math.e)
LANES = 128
V7X_VMEM_LIMIT = 56 * 1024 * 1024

PROJ_SIZES = (256, 128, 32, 256, 256, 256, 256, 256, 256, 256, 256, 256)

_M_ACQ, _M_ACKV, _M_BQ, _M_BK, _M_BV = 0, 256, 384, 640, 896
_M_CQ, _M_CQR, _M_CV = 1152, 1408, 1664
_M_DQ, _M_DQR, _M_DK, _M_DKR, _M_DV = 1920, 2176, 2432, 2688, 2944
_M_COLS = 3200
_T_CK, _T_CKR, _T_AKR, _T_AKRR, _T_ROWS = 0, 256, 512, 544, 576


def _cparams(sem):
    return pltpu.CompilerParams(dimension_semantics=sem, vmem_limit_bytes=V7X_VMEM_LIMIT)


def _rms(x):
    return x * lax.rsqrt(jnp.mean(x * x, axis=-1, keepdims=True) + RMS_EPS)


def _nt_dot(a, b):
    return lax.dot_general(a, b, (((1,), (1,)), ((), ())), preferred_element_type=F32)


def _head_of_lane(shape):
    return lax.broadcasted_iota(jnp.int32, shape, len(shape) - 1) // HEAD_DIM


def _proj_kernel(x_ref, g1_ref, wm_ref, wt_ref, gq_ref, wuq_ref, gkv_ref, wukt_ref, wuv_ref,
                 tab_ref, tabt_ref,
                 qa_ref, kat_ref, va_ref, qb_ref, kb_ref, vb_ref, qc_ref, kct_ref, vc_ref,
                 qd_ref, kd_ref, vd_ref):
    x = x_ref[...]
    h = (_rms(x) * g1_ref[...]).astype(BF16)
    p = jnp.dot(h, wm_ref[...], preferred_element_type=F32)
    pt = _nt_dot(wt_ref[...], h)
    tab = tab_ref[...]
    cos_a, sin_a, cos_c, sin_c, cos_d, sin_d = (tab[:, LANES * i:LANES * (i + 1)] for i in range(6))
    tabt = tabt_ref[...]
    cos_t, sin_t = tabt[0:A_ROPE], tabt[A_ROPE:2 * A_ROPE]

    scale_a = LOG2E * (A_NOPE + A_ROPE) ** -0.5
    latq = (_rms(p[:, _M_ACQ:_M_ACQ + A_Q_LORA]) * gq_ref[...]).astype(BF16)
    qa2 = jnp.dot(latq, wuq_ref[...], preferred_element_type=F32)
    for hd in range(N_HEADS):
        lo = A_PAD * hd
        blk = qa2[:, lo:lo + A_PAD] * cos_a + qa2[:, N_HEADS * A_PAD + lo:N_HEADS * A_PAD + lo + A_PAD] * sin_a
        qa_ref[:, lo:lo + A_PAD] = (blk * scale_a).astype(BF16)
    latkv = (_rms(p[:, _M_ACKV:_M_ACKV + A_KV_LORA]) * gkv_ref[...]).astype(BF16)
    va_ref[...] = jnp.dot(latkv, wuv_ref[...], preferred_element_type=F32).astype(BF16)
    knt = _nt_dot(wukt_ref[...], latkv)
    krt = (pt[_T_AKR:_T_AKR + A_ROPE] * cos_t + pt[_T_AKRR:_T_AKRR + A_ROPE] * sin_t).astype(BF16)
    tm = x.shape[0]
    for hd in range(N_HEADS):
        lo = A_PAD * hd
        kat_ref[lo:lo + A_NOPE, :] = knt[A_NOPE * hd:A_NOPE * (hd + 1)].astype(BF16)
        kat_ref[lo + A_NOPE:lo + A_NOPE + A_ROPE, :] = krt
        kat_ref[lo + A_NOPE + A_ROPE:lo + A_PAD, :] = jnp.zeros((A_PAD - A_NOPE - A_ROPE, tm), BF16)

    qb_ref[...] = (p[:, _M_BQ:_M_BQ + MIX_COLS] * (LOG2E * HEAD_DIM ** -0.5)).astype(BF16)
    kb_ref[...] = p[:, _M_BK:_M_BK + MIX_COLS].astype(BF16)
    vb_ref[...] = p[:, _M_BV:_M_BV + MIX_COLS].astype(BF16)

    scale_c = LOG2E * C_DIM ** -0.5
    for j in range(MIX_COLS // LANES):
        lo = LANES * j
        blk = p[:, _M_CQ + lo:_M_CQ + lo + LANES] * cos_c + p[:, _M_CQR + lo:_M_CQR + lo + LANES] * sin_c
        qc_ref[:, lo:lo + LANES] = (blk * scale_c).astype(BF16)
    reps = MIX_COLS // A_ROPE
    cos_ct = jnp.concatenate([cos_t] * reps, axis=0)
    sin_ct = jnp.concatenate([sin_t] * reps, axis=0)
    kct_ref[...] = (pt[_T_CK:_T_CK + MIX_COLS] * cos_ct + pt[_T_CKR:_T_CKR + MIX_COLS] * sin_ct).astype(BF16)
    vc_ref[...] = p[:, _M_CV:_M_CV + MIX_COLS].astype(BF16)

    for j in range(MIX_COLS // LANES):
        lo = LANES * j
        qblk = p[:, _M_DQ + lo:_M_DQ + lo + LANES] * cos_d + p[:, _M_DQR + lo:_M_DQR + lo + LANES] * sin_d
        qd_ref[:, lo:lo + LANES] = (qblk * (LOG2E * HEAD_DIM ** -0.5)).astype(BF16)
        kblk = p[:, _M_DK + lo:_M_DK + lo + LANES] * cos_d + p[:, _M_DKR + lo:_M_DKR + lo + LANES] * sin_d
        kd_ref[:, lo:lo + LANES] = kblk.astype(BF16)
    vd_ref[...] = p[:, _M_DV:_M_DV + MIX_COLS].astype(BF16)


def _proj_call(x2d, t_len, g1, wm, wt, gq, wuq, gkv, wukt, wuv, tab, tabt, tm):
    n = x2d.shape[0]
    nt = t_len // tm
    full = lambda a: pl.BlockSpec(a.shape, lambda i: (0,) * a.ndim)
    row = lambda w: pl.BlockSpec((tm, w), lambda i: (i, 0))
    col = lambda r: pl.BlockSpec((r, tm), lambda i: (0, i))
    tok = lambda w: jax.ShapeDtypeStruct((n, w), BF16)
    out_shape = (tok(N_HEADS * A_PAD), jax.ShapeDtypeStruct((N_HEADS * A_PAD, n), BF16), tok(MIX_COLS),
                 tok(MIX_COLS), tok(MIX_COLS), tok(MIX_COLS),
                 tok(MIX_COLS), jax.ShapeDtypeStruct((MIX_COLS, n), BF16), tok(MIX_COLS),
                 tok(MIX_COLS), tok(MIX_COLS), tok(MIX_COLS))
    out_specs = (row(N_HEADS * A_PAD), col(N_HEADS * A_PAD), row(MIX_COLS),
                 row(MIX_COLS), row(MIX_COLS), row(MIX_COLS),
                 row(MIX_COLS), col(MIX_COLS), row(MIX_COLS),
                 row(MIX_COLS), row(MIX_COLS), row(MIX_COLS))
    in_specs = [row(D_MODEL), full(g1), full(wm), full(wt), full(gq), full(wuq), full(gkv), full(wukt), full(wuv),
                pl.BlockSpec((tm, tab.shape[1]), lambda i: (i % nt, 0)),
                pl.BlockSpec((tabt.shape[0], tm), lambda i: (0, i % nt))]
    return pl.pallas_call(
        _proj_kernel, grid=(n // tm,), in_specs=in_specs, out_specs=out_specs, out_shape=out_shape,
        compiler_params=_cparams(("parallel",)), name="proj_in",
    )(x2d, g1, wm, wt, gq, wuq, gkv, wukt, wuv, tab, tabt)


def _row_total(l_lanes):
    return jnp.sum(l_lanes, axis=1, keepdims=True)


def _online_step(s, v, m_ref, l_ref, acc_ref, idx):
    m_prev = m_ref[idx]
    m_new = jnp.maximum(m_prev, jnp.max(s, axis=1, keepdims=True))
    alpha = jnp.exp2(m_prev - m_new)
    p = jnp.exp2(s - m_new)
    part = p[:, 0:LANES]
    for c in range(1, s.shape[1] // LANES):
        part = part + p[:, LANES * c:LANES * (c + 1)]
    l_ref[idx] = alpha * l_ref[idx] + part
    acc_ref[idx] = alpha * acc_ref[idx] + jnp.dot(p.astype(BF16), v, preferred_element_type=F32)
    m_ref[idx] = m_new


def _mla_kernel(q_ref, kt_ref, v_ref, o_ref, m_ref, l_ref, acc_ref):
    j = pl.program_id(2)

    @pl.when(j == 0)
    def _():
        m_ref[...] = jnp.full(m_ref.shape, -jnp.inf, F32)
        l_ref[...] = jnp.zeros(l_ref.shape, F32)
        acc_ref[...] = jnp.zeros(acc_ref.shape, F32)

    v = v_ref[...]
    for hd in range(N_HEADS):
        lo = A_PAD * hd
        s = jnp.dot(q_ref[:, lo:lo + A_PAD], kt_ref[lo:lo + A_PAD, :], preferred_element_type=F32)
        _online_step(s, v, m_ref, l_ref, acc_ref, hd)

    @pl.when(j == pl.num_programs(2) - 1)
    def _():
        head = _head_of_lane(o_ref.shape)
        out = jnp.zeros(o_ref.shape, F32)
        for hd in range(N_HEADS):
            out = out + jnp.where(head == hd, acc_ref[hd] * (1.0 / _row_total(l_ref[hd])), 0.0)
        o_ref[...] = out.astype(o_ref.dtype)


def _mla_call(q, kt, v, bsz, t_len, tq, tk):
    n = q.shape[0]
    nq, nk = t_len // tq, t_len // tk
    return pl.pallas_call(
        _mla_kernel, grid=(bsz, nq, nk),
        in_specs=[pl.BlockSpec((tq, N_HEADS * A_PAD), lambda b, i, j: (b * nq + i, 0)),
                  pl.BlockSpec((N_HEADS * A_PAD, tk), lambda b, i, j: (0, b * nk + j)),
                  pl.BlockSpec((tk, MIX_COLS), lambda b, i, j: (b * nk + j, 0))],
        out_specs=pl.BlockSpec((tq, MIX_COLS), lambda b, i, j: (b * nq + i, 0)),
        out_shape=jax.ShapeDtypeStruct((n, MIX_COLS), BF16),
        scratch_shapes=[pltpu.VMEM((N_HEADS, tq, 1), F32), pltpu.VMEM((N_HEADS, tq, LANES), F32),
                        pltpu.VMEM((N_HEADS, tq, MIX_COLS), F32)],
        compiler_params=_cparams(("parallel", "parallel", "arbitrary")), name="attn_mla",
    )(q, kt, v)


def _diff_kernel(lam_init, q_ref, kt_ref, v_ref, lam_ref, g_ref, o_ref, qm_ref, m_ref, l_ref, acc_ref):
    j = pl.program_id(2)
    n_maps = 2 * N_HEADS

    @pl.when(j == 0)
    def _():
        m_ref[...] = jnp.full(m_ref.shape, -jnp.inf, F32)
        l_ref[...] = jnp.zeros(l_ref.shape, F32)
        acc_ref[...] = jnp.zeros(acc_ref.shape, F32)
        q = q_ref[...]
        group = lax.broadcasted_iota(jnp.int32, q.shape, 1) // C_DIM
        for mi in range(n_maps):
            qm_ref[mi] = jnp.where(group == mi, q, jnp.zeros_like(q))

    v = v_ref[...]
    kt = kt_ref[...]
    for mi in range(n_maps):
        s = jnp.dot(qm_ref[mi], kt, preferred_element_type=F32)
        _online_step(s, v, m_ref, l_ref, acc_ref, mi)

    @pl.when(j == pl.num_programs(2) - 1)
    def _():
        lam = lam_ref[...]
        head = _head_of_lane(o_ref.shape)
        o = jnp.zeros(o_ref.shape, F32)
        for hd in range(N_HEADS):
            oh = (acc_ref[2 * hd] * (1.0 / _row_total(l_ref[2 * hd]))
                  - lam * (acc_ref[2 * hd + 1] * (1.0 / _row_total(l_ref[2 * hd + 1]))))
            o = o + jnp.where(head == hd, oh, 0.0)
        o2 = o * o
        inv = jnp.zeros(o_ref.shape, F32)
        for hd in range(N_HEADS):
            ms = jnp.sum(jnp.where(head == hd, o2, 0.0), axis=1, keepdims=True) * (1.0 / HEAD_DIM)
            inv = inv + jnp.where(head == hd, lax.rsqrt(ms + RMS_EPS), 0.0)
        o_ref[...] = ((o * inv * g_ref[...]) * (1.0 - lam_init)).astype(o_ref.dtype)


def _diff_call(q, kt, v, lam, g_tiled, lam_init, bsz, t_len, tq, tk):
    n = q.shape[0]
    nq, nk = t_len // tq, t_len // tk
    n_maps = 2 * N_HEADS
    return pl.pallas_call(
        functools.partial(_diff_kernel, lam_init), grid=(bsz, nq, nk),
        in_specs=[pl.BlockSpec((tq, MIX_COLS), lambda b, i, j: (b * nq + i, 0)),
                  pl.BlockSpec((MIX_COLS, tk), lambda b, i, j: (0, b * nk + j)),
                  pl.BlockSpec((tk, MIX_COLS), lambda b, i, j: (b * nk + j, 0)),
                  pl.BlockSpec((1, 1), lambda b, i, j: (0, 0)),
                  pl.BlockSpec((1, MIX_COLS), lambda b, i, j: (0, 0))],
        out_specs=pl.BlockSpec((tq, MIX_COLS), lambda b, i, j: (b * nq + i, 0)),
        out_shape=jax.ShapeDtypeStruct((n, MIX_COLS), BF16),
        scratch_shapes=[pltpu.VMEM((n_maps, tq, MIX_COLS), BF16),
                        pltpu.VMEM((n_maps, tq, 1), F32), pltpu.VMEM((n_maps, tq, LANES), F32),
                        pltpu.VMEM((n_maps, tq, MIX_COLS), F32)],
        compiler_params=_cparams(("parallel", "parallel", "arbitrary")), name="attn_diff",
    )(q, kt, v, lam, g_tiled)


def _window_heads(q, kw, vw, add, mult, o_ref):
    head = _head_of_lane(q.shape)
    out = jnp.zeros(q.shape, F32)
    for hd in range(N_HEADS):
        qh = jnp.where(head == hd, q, jnp.zeros_like(q))
        s = _nt_dot(qh, kw)
        if add is not None:
            s = s + add(hd)
        if mult is not None:
            s = jnp.where(mult > 0.0, s, NEG_BIG)
        m = jnp.max(s, axis=1, keepdims=True)
        p = jnp.exp2(s - m)
        if mult is not None:
            p = p * mult
        l = jnp.sum(p, axis=1, keepdims=True)
        o = jnp.dot(p.astype(BF16), vw, preferred_element_type=F32)
        out = out + jnp.where(head == hd, o * (1.0 / l), 0.0)
    o_ref[...] = out.astype(o_ref.dtype)


def _nbr_kernel(nb, q_ref, k_ref, v_ref, bias_ref, o_ref):
    i = pl.program_id(1)
    start = pl.multiple_of(jnp.clip(i - 2, 0, nb - B_WIN // QBLK) * QBLK, QBLK)
    kw = k_ref[pl.ds(start, B_WIN), :]
    vw = v_ref[pl.ds(start, B_WIN), :]
    _window_heads(q_ref[...], kw, vw, lambda hd: bias_ref[0, hd], None, o_ref)


def _nbr_block_type(i, nb):
    return jnp.where(i < 2, i, jnp.where(i > nb - 3, i - nb + 5, 2))


def _nbr_call(q, k, v, bias, bsz, t_len):
    n = q.shape[0]
    nb = t_len // QBLK
    return pl.pallas_call(
        functools.partial(_nbr_kernel, nb), grid=(bsz, nb),
        in_specs=[pl.BlockSpec((QBLK, MIX_COLS), lambda b, i: (b * nb + i, 0)),
                  pl.BlockSpec((t_len, MIX_COLS), lambda b, i: (b, 0)),
                  pl.BlockSpec((t_len, MIX_COLS), lambda b, i: (b, 0)),
                  pl.BlockSpec((1, N_HEADS, QBLK, B_WIN), lambda b, i: (_nbr_block_type(i, nb), 0, 0, 0))],
        out_specs=pl.BlockSpec((QBLK, MIX_COLS), lambda b, i: (b * nb + i, 0)),
        out_shape=jax.ShapeDtypeStruct((n, MIX_COLS), BF16),
        compiler_params=_cparams(("parallel", "arbitrary")), name="attn_nbr",
    )(q, k, v, bias)


def _nbr_bias_tables(rpb, t_len):
    rows = t_len // GRID_W
    nb = t_len // QBLK
    kr = min(NA_ROWS, rows)
    reps = (0, 1, 2, nb - 2, nb - 1)
    n_dc = 2 * NA_COLS - 1
    flat, valid = [], []
    for b in reps:
        t = b * QBLK + np.arange(QBLK)
        r, c = (t // GRID_W)[:, None], (t % GRID_W)[:, None]
        rs = np.clip(r - kr // 2, 0, rows - kr)
        cs = np.clip(c - NA_COLS // 2, 0, GRID_W - NA_COLS)
        start = int(np.clip(b - 2, 0, nb - B_WIN // QBLK)) * QBLK
        key = start + np.arange(B_WIN)[None, :]
        key_r, key_c = key // GRID_W, key % GRID_W
        ok = (key_r >= rs) & (key_r < rs + kr) & (key_c >= cs) & (key_c < cs + NA_COLS)
        dr = np.clip(key_r - r + (NA_ROWS - 1), 0, 2 * NA_ROWS - 2)
        dc = np.clip(key_c - c + (NA_COLS - 1), 0, n_dc - 1)
        flat.append(dr * n_dc + dc)
        valid.append(ok)
    flat, valid = np.stack(flat), np.stack(valid)
    vals = jnp.take(rpb.reshape(N_HEADS, -1).astype(F32), flat.reshape(-1), axis=1)
    vals = vals.reshape(N_HEADS, len(reps), QBLK, B_WIN) * LOG2E
    return jnp.transpose(jnp.where(valid[None], vals, NEG_BIG), (1, 0, 2, 3))


def _dil_kernel(t_len, tq, q_ref, k_ref, v_ref, o_ref):
    i = pl.program_id(1)
    w = tq + 2 * D_REACH
    t0 = i * tq
    start = pl.multiple_of(jnp.clip(t0 - D_REACH, 0, t_len - w), tq)
    kw = k_ref[pl.ds(start, w), :]
    vw = v_ref[pl.ds(start, w), :]
    d = (start - t0) + lax.broadcasted_iota(jnp.int32, (tq, w), 1) - lax.broadcasted_iota(jnp.int32, (tq, w), 0)
    ad = jnp.abs(d)
    mult = ((ad <= 64).astype(F32)
            + (((d & 3) == 0) & (ad <= 256)).astype(F32)
            + (((d & 15) == 0) & (ad <= D_REACH)).astype(F32))
    _window_heads(q_ref[...], kw, vw, None, mult, o_ref)


def _dil_call(q, k, v, bsz, t_len, tq):
    n = q.shape[0]
    nq = t_len // tq
    return pl.pallas_call(
        functools.partial(_dil_kernel, t_len, tq), grid=(bsz, nq),
        in_specs=[pl.BlockSpec((tq, MIX_COLS), lambda b, i: (b * nq + i, 0)),
                  pl.BlockSpec((t_len, MIX_COLS), lambda b, i: (b, 0)),
                  pl.BlockSpec((t_len, MIX_COLS), lambda b, i: (b, 0))],
        out_specs=pl.BlockSpec((tq, MIX_COLS), lambda b, i: (b * nq + i, 0)),
        out_shape=jax.ShapeDtypeStruct((n, MIX_COLS), BF16),
        compiler_params=_cparams(("parallel", "arbitrary")), name="attn_dil",
    )(q, k, v)


def _out_kernel(oa_ref, ob_ref, oc_ref, od_ref, x_ref, wo_ref, g2_ref, wr_ref, x2_ref, hn_ref, aff_ref):
    acc = x_ref[...]
    for mi, o_ref in enumerate((oa_ref, ob_ref, oc_ref, od_ref)):
        acc = acc + jnp.dot(o_ref[...], wo_ref[MIX_COLS * mi:MIX_COLS * (mi + 1), :], preferred_element_type=F32)
    x2_ref[...] = acc
    hn = _rms(acc) * g2_ref[...]
    hn_ref[...] = hn.astype(BF16)
    logits = jnp.dot(hn, wr_ref[...], preferred_element_type=F32, precision=lax.Precision.HIGHEST)
    lane = lax.broadcasted_iota(jnp.int32, logits.shape, 1)
    logits = jnp.where(lane < N_EXPERTS, logits, -jnp.inf)
    m = jnp.max(logits, axis=1, keepdims=True)
    e = jnp.exp(logits - m)
    aff_ref[...] = e / jnp.sum(e, axis=1, keepdims=True)


def _out_call(oa, ob, oc, od, x2d, wo, g2, wr, tm):
    n = x2d.shape[0]
    full = lambda a: pl.BlockSpec(a.shape, lambda i: (0,) * a.ndim)
    row = lambda w: pl.BlockSpec((tm, w), lambda i: (i, 0))
    return pl.pallas_call(
        _out_kernel, grid=(n // tm,),
        in_specs=[row(MIX_COLS)] * 4 + [row(D_MODEL), full(wo), full(g2), full(wr)],
        out_specs=(row(D_MODEL), row(D_MODEL), row(LANES)),
        out_shape=(jax.ShapeDtypeStruct((n, D_MODEL), F32), jax.ShapeDtypeStruct((n, D_MODEL), BF16),
                   jax.ShapeDtypeStruct((n, LANES), F32)),
        compiler_params=_cparams(("parallel",)), name="proj_out",
    )(oa, ob, oc, od, x2d, wo, g2, wr)


FLAG_FIRST, FLAG_LAST, FLAG_VALID = 1, 2, 4
Y_BLK = 128


def _item_lists(cnt, lo, hi, n_items):
    ends = jnp.cumsum(cnt)
    starts = ends - cnt
    w = jnp.arange(n_items, dtype=jnp.int32)
    grp = jnp.minimum(jnp.searchsorted(ends, w, side="right"), cnt.shape[0] - 1).astype(jnp.int32)
    rank = w - starts[grp]
    val = jnp.minimum(lo[grp] + rank, hi[grp])
    return grp, val, rank, w < ends[-1], starts, ends


def _route_tables(aff, n, cap, ts, tc, tt):
    gate, idx = lax.top_k(aff[:, :N_EXPERTS].T, cap)
    idx, gate = lax.sort((idx, gate), dimension=1, num_keys=1)
    idx = idx.astype(jnp.int32)

    tiles_per_e = cap // ts
    n_tiles = N_EXPERTS * tiles_per_e
    c0 = (idx[:, ::ts] // tc).reshape(n_tiles)
    c1 = (idx[:, ts - 1::ts] // tc).reshape(n_tiles)
    cnt = c1 - c0 + 1
    n_g = N_EXPERTS * (n // tc) + n_tiles
    g_tile, g_chunk, g_rank, g_valid, _, _ = _item_lists(cnt, c0, c1, n_g)
    g_flags = (jnp.where(g_valid & (g_rank == 0), FLAG_FIRST, 0)
               | jnp.where(g_valid & (g_rank == cnt[g_tile] - 1), FLAG_LAST, 0)
               | jnp.where(g_valid, FLAG_VALID, 0)).astype(jnp.int32)

    n_tt = n // tt
    blk_per_e = cap // Y_BLK
    bounds = jnp.arange(n_tt + 1, dtype=jnp.int32) * tt
    pos = jax.vmap(lambda row: jnp.searchsorted(row, bounds, side="left"))(idx).astype(jnp.int32)
    lo, hi = pos[:, :-1], pos[:, 1:]
    b0 = jnp.minimum(lo // Y_BLK, blk_per_e - 1)
    b1 = jnp.maximum((hi - 1) // Y_BLK, b0)
    ccnt = jnp.where(hi > lo, b1 - b0 + 1, 0)
    ccnt = ccnt.at[0].set(jnp.maximum(ccnt[0], 1))
    base = (jnp.arange(N_EXPERTS, dtype=jnp.int32) * blk_per_e)[:, None]
    n_c = N_EXPERTS * blk_per_e + (N_EXPERTS + 1) * n_tt
    pair, c_blk, _, c_valid, starts, ends = _item_lists(ccnt.T.reshape(-1), (b0 + base).T.reshape(-1),
                                                        (b1 + base).T.reshape(-1), n_c)
    c_tile = pair // N_EXPERTS
    w = jnp.arange(n_c, dtype=jnp.int32)
    c_flags = (jnp.where(c_valid & (w == starts[c_tile * N_EXPERTS]), FLAG_FIRST, 0)
               | jnp.where(c_valid & (w == ends[c_tile * N_EXPERTS + N_EXPERTS - 1] - 1), FLAG_LAST, 0)
               | jnp.where(c_valid, FLAG_VALID, 0)).astype(jnp.int32)
    return idx, gate, (g_tile, g_chunk, g_flags), (c_tile, c_blk, c_flags)


def _ffn_kernel(tiles_per_e, tc, tile_ref, chunk_ref, flag_ref, tok_ref, gate_ref, hn_ref, wg_ref, wu_ref, wd_ref,
                yhi_ref, ylo_ref, xacc, wgb, wub, wdb):
    w = pl.program_id(0)
    flags = flag_ref[w]
    first = (flags & FLAG_FIRST) != 0

    @pl.when(first & (tile_ref[w] % tiles_per_e == 0))
    def _():
        wgb[...] = wg_ref[0, 0].astype(BF16)
        wub[...] = wu_ref[0, 0].astype(BF16)
        wdb[...] = wd_ref[0, 0].astype(BF16)

    @pl.when(first)
    def _():
        xacc[...] = jnp.zeros(xacc.shape, F32)

    @pl.when((flags & FLAG_VALID) != 0)
    def _():
        ts = tok_ref.shape[0]
        token = chunk_ref[w] * tc + lax.broadcasted_iota(jnp.int32, (ts, tc), 1)
        onehot = jnp.where(tok_ref[...] == token, 1.0, 0.0).astype(BF16)
        xacc[...] += jnp.dot(onehot, hn_ref[...], preferred_element_type=F32)

    @pl.when((flags & FLAG_LAST) != 0)
    def _():
        xg = xacc[...].astype(BF16)
        a = jnp.dot(xg, wgb[...], preferred_element_type=F32)
        b = jnp.dot(xg, wub[...], preferred_element_type=F32)
        hid = (a * jax.nn.sigmoid(a) * b).astype(BF16)
        y = jnp.dot(hid, wdb[...], preferred_element_type=F32) * gate_ref[...]
        hi = y.astype(BF16)
        yhi_ref[...] = hi
        ylo_ref[...] = (y - hi.astype(F32)).astype(BF16)


def _ffn_call(items, idx, gate, hn, w_gate, w_up, w_down, layer, ts, tc):
    n_exp, cap = idx.shape
    tiles_per_e = cap // ts
    tile, chunk, flags = items
    tok = idx.reshape(n_exp * cap, 1)
    gate2 = gate.reshape(n_exp * cap, 1)
    slot = lambda wd: pl.BlockSpec((ts, wd), lambda w, tile, chunk, flags: (tile[w], 0))
    wspec = pl.BlockSpec((1, 1, D_MODEL, D_MODEL), lambda w, tile, chunk, flags: (layer, tile[w] // tiles_per_e, 0, 0))
    grid_spec = pltpu.PrefetchScalarGridSpec(
        num_scalar_prefetch=3, grid=(tile.shape[0],),
        in_specs=[slot(1), slot(1),
                  pl.BlockSpec((tc, D_MODEL), lambda w, tile, chunk, flags: (chunk[w], 0)),
                  wspec, wspec, wspec],
        out_specs=(slot(D_MODEL), slot(D_MODEL)),
        scratch_shapes=[pltpu.VMEM((ts, D_MODEL), F32),
                        pltpu.VMEM((D_MODEL, D_MODEL), BF16), pltpu.VMEM((D_MODEL, D_MODEL), BF16),
                        pltpu.VMEM((D_MODEL, D_MODEL), BF16)])
    out = jax.ShapeDtypeStruct((n_exp * cap, D_MODEL), BF16)
    return pl.pallas_call(
        functools.partial(_ffn_kernel, tiles_per_e, tc), grid_spec=grid_spec, out_shape=(out, out),
        compiler_params=_cparams(("arbitrary",)), name="expert_ffn",
    )(tile, chunk, flags, tok, gate2, hn, w_gate, w_up, w_down)


def _combine_kernel(tt, final, tile_ref, blk_ref, flag_ref, tok_ref, yhi_ref, ylo_ref, x_ref, g_ref, o_ref):
    w = pl.program_id(0)
    flags = flag_ref[w]

    @pl.when((flags & FLAG_FIRST) != 0)
    def _():
        o_ref[...] = x_ref[...]

    @pl.when((flags & FLAG_VALID) != 0)
    def _():
        token = tile_ref[w] * tt + lax.broadcasted_iota(jnp.int32, (tt, Y_BLK), 0)
        onehot = jnp.where(token == tok_ref[0], 1.0, 0.0).astype(BF16)
        o_ref[...] += (jnp.dot(onehot, yhi_ref[...], preferred_element_type=F32)
                       + jnp.dot(onehot, ylo_ref[...], preferred_element_type=F32))

    if final:
        @pl.when((flags & FLAG_LAST) != 0)
        def _():
            o_ref[...] = _rms(o_ref[...]) * g_ref[...]


def _combine_call(items, idx, yhi, ylo, x2, g_final, tt, final):
    n = x2.shape[0]
    tile, blk, flags = items
    tok = idx.reshape(-1, 1, Y_BLK)
    yspec = pl.BlockSpec((Y_BLK, D_MODEL), lambda w, tile, blk, flags: (blk[w], 0))
    xspec = pl.BlockSpec((tt, D_MODEL), lambda w, tile, blk, flags: (tile[w], 0))
    grid_spec = pltpu.PrefetchScalarGridSpec(
        num_scalar_prefetch=3, grid=(tile.shape[0],),
        in_specs=[pl.BlockSpec((1, 1, Y_BLK), lambda w, tile, blk, flags: (blk[w], 0, 0)), yspec, yspec, xspec,
                  pl.BlockSpec((1, D_MODEL), lambda w, tile, blk, flags: (0, 0))],
        out_specs=xspec)
    return pl.pallas_call(
        functools.partial(_combine_kernel, tt, final), grid_spec=grid_spec,
        out_shape=jax.ShapeDtypeStruct((n, D_MODEL), F32),
        compiler_params=_cparams(("arbitrary",)), name="expert_combine",
    )(tile, blk, flags, tok, yhi, ylo, x2, g_final)


def _rot_cols(w, d):
    k, c = w.shape
    half = d // 2
    wg = w.reshape(k, c // d, 2, half)
    return jnp.concatenate([-wg[:, :, 1], wg[:, :, 0]], axis=2).reshape(k, c)


def _rope_tables(t_len):
    pos = jnp.arange(t_len, dtype=F32)

    def cs(d):
        half = d // 2
        inv = ROPE_THETA ** (-jnp.arange(half, dtype=F32) / half)
        ang = pos[:, None] * inv[None, :]
        return (jnp.concatenate([jnp.cos(ang)] * 2, axis=1), jnp.concatenate([jnp.sin(ang)] * 2, axis=1))

    c32, s32 = cs(A_ROPE)
    c64, s64 = cs(HEAD_DIM)
    ones = jnp.ones((t_len, A_NOPE), F32)
    zeros = jnp.zeros((t_len, A_NOPE), F32)
    pad = jnp.zeros((t_len, A_PAD - A_NOPE - A_ROPE), F32)
    tab = jnp.concatenate([ones, c32, pad, zeros, s32, pad,
                           jnp.tile(c32, (1, LANES // A_ROPE)), jnp.tile(s32, (1, LANES // A_ROPE)),
                           jnp.tile(c64, (1, LANES // HEAD_DIM)), jnp.tile(s64, (1, LANES // HEAD_DIM))], axis=1)
    tabt = jnp.concatenate([c32.T, s32.T], axis=0)
    return tab, tabt


def _layer_weights(l, w_in, a_w_uq, a_w_ukv, w_out, w_router):
    pts = np.cumsum(PROJ_SIZES)[:-1]
    (a_cq, a_ckv, a_kr, b_q, b_k, b_v, c_q, c_k, c_v, d_q, d_k, d_v) = jnp.split(w_in[l], pts, axis=1)
    wm = jnp.concatenate([a_cq, a_ckv, b_q, b_k, b_v, c_q, _rot_cols(c_q, C_DIM), c_v,
                          d_q, _rot_cols(d_q, HEAD_DIM), d_k, _rot_cols(d_k, HEAD_DIM), d_v], axis=1).astype(BF16)
    wt = jnp.concatenate([c_k, _rot_cols(c_k, C_DIM), a_kr, _rot_cols(a_kr, A_ROPE)], axis=1).T.astype(BF16)
    uq = a_w_uq[l].reshape(A_Q_LORA, N_HEADS, A_NOPE + A_ROPE)
    zpad = jnp.zeros((A_Q_LORA, N_HEADS, A_PAD - A_NOPE - A_ROPE), F32)
    uq_rope = uq[:, :, A_NOPE:]
    uq_rot = _rot_cols(uq_rope.reshape(A_Q_LORA, N_HEADS * A_ROPE), A_ROPE).reshape(A_Q_LORA, N_HEADS, A_ROPE)
    plain = jnp.concatenate([uq, zpad], axis=2).reshape(A_Q_LORA, N_HEADS * A_PAD)
    rot = jnp.concatenate([jnp.zeros_like(uq[:, :, :A_NOPE]), uq_rot, zpad], axis=2).reshape(A_Q_LORA, N_HEADS * A_PAD)
    wuq = jnp.concatenate([plain, rot], axis=1).astype(BF16)
    ukv = a_w_ukv[l].reshape(A_KV_LORA, N_HEADS, 2 * HEAD_DIM)
    wukt = ukv[:, :, :A_NOPE].reshape(A_KV_LORA, MIX_COLS).T.astype(BF16)
    wuv = ukv[:, :, A_NOPE:].reshape(A_KV_LORA, MIX_COLS).astype(BF16)
    wo = w_out[l].astype(BF16)
    wr = jnp.concatenate([w_router[l], jnp.zeros((D_MODEL, LANES - N_EXPERTS), F32)], axis=1)
    return wm, wt, wuq, wukt, wuv, wo, wr


def _trunk(x, norm1_g, w_in, a_qnorm_g, a_w_uq, a_kvnorm_g, a_w_ukv, b_rpb, c_lambda, c_subln_g, w_out,
           norm2_g, w_router, w_gate, w_up, w_down, final_g):
    bsz, t_len, _ = x.shape
    n = bsz * t_len
    tm = 512
    tq = 512
    tk_mla, tk_diff = min(4096, t_len), 2048
    tq_dil = 256
    ts, tc, tt = 256, 1024, 512
    depth = w_in.shape[0]
    cap = CAP_FACTOR * n // N_EXPERTS
    assert t_len % tk_mla == 0 and t_len % tk_diff == 0 and t_len >= tq_dil + 2 * D_REACH and t_len // QBLK >= 5
    assert cap % ts == 0 and n % tc == 0 and n % tt == 0
    tab, tabt = _rope_tables(t_len)
    x2d = x.reshape(n, D_MODEL)
    g_final = final_g.reshape(1, D_MODEL)
    for l in range(depth):
        lam_init = 0.8 - 0.6 * math.exp(-0.3 * l)
        wm, wt, wuq, wukt, wuv, wo, wr = _layer_weights(l, w_in, a_w_uq, a_w_ukv, w_out, w_router)
        row = lambda g: g[l].reshape(1, -1)
        (qa, kat, va, qb, kb, vb, qc, kct, vc, qd, kd, vd) = _proj_call(
            x2d, t_len, row(norm1_g), wm, wt, row(a_qnorm_g), wuq, row(a_kvnorm_g), wukt, wuv, tab, tabt, tm)
        oa = _mla_call(qa, kat, va, bsz, t_len, tq, tk_mla)
        ob = _nbr_call(qb, kb, vb, _nbr_bias_tables(b_rpb[l], t_len), bsz, t_len)
        lp = c_lambda[l].astype(F32)
        lam = (jnp.exp(jnp.sum(lp[0] * lp[1])) - jnp.exp(jnp.sum(lp[2] * lp[3])) + lam_init).reshape(1, 1)
        g_sub = jnp.tile(c_subln_g[l], N_HEADS).reshape(1, MIX_COLS)
        oc = _diff_call(qc, kct, vc, lam, g_sub, lam_init, bsz, t_len, tq, tk_diff)
        od = _dil_call(qd, kd, vd, bsz, t_len, tq_dil)
        x2, hn, aff = _out_call(oa, ob, oc, od, x2d, wo, row(norm2_g), wr, tm)
        idx, gate, g_items, c_items = _route_tables(aff, n, cap, ts, tc, tt)
        yhi, ylo = _ffn_call(g_items, idx, gate, hn, w_gate, w_up, w_down, l, ts, tc)
        x2d = _combine_call(c_items, idx, yhi, ylo, x2, g_final, tt, l == depth - 1)
    return x2d.reshape(bsz, t_len, D_MODEL)


def kernel(x_prompt, x_sample, norm1_g, w_in, a_qnorm_g, a_w_uq, a_kvnorm_g, a_w_ukv, b_rpb, c_lambda, c_subln_g,
           w_out, norm2_g, w_router, w_gate, w_up, w_down, final_g):
    params = (norm1_g, w_in, a_qnorm_g, a_w_uq, a_kvnorm_g, a_w_ukv, b_rpb, c_lambda, c_subln_g, w_out,
              norm2_g, w_router, w_gate, w_up, w_down, final_g)
    return (_trunk(x_prompt, *params), _trunk(x_sample, *params))
```

```python
import functools
import math

import numpy as np
import jax
import jax.numpy as jnp
from jax import lax
from jax.experimental import pallas as pl
from jax.experimental.pallas import tpu as pltpu

F32 = jnp.float32
BF16 = jnp.bfloat16

D_MODEL = 1024
N_HEADS = 4
HEAD_DIM = 64
MIX_COLS = N_HEADS * HEAD_DIM
A_NOPE = 64
A_ROPE = 32
A_PAD = 128
A_Q_LORA = 256
A_KV_LORA = 128
C_DIM = 32
GRID_W = 64
NA_ROWS = 8
NA_COLS = 16
QBLK = 128
B_WIN = 5 * QBLK
D_REACH = 1024
N_EXPERTS = 16
CAP_FACTOR = 2
ROPE_THETA = 10000.0
RMS_EPS = 1e-6
NEG_BIG = -1e30
LOG2E = math.log2(math.e)
LANES = 128
V7X_VMEM_LIMIT = 56 * 1024 * 1024

PROJ_SIZES = (256, 128, 32, 256, 256, 256, 256, 256, 256, 256, 256, 256)

_M_ACQ, _M_ACKV, _M_BQ, _M_BK, _M_BV = 0, 256, 384, 640, 896
_M_CQ, _M_CQR, _M_CV = 1152, 1408, 1664
_M_DQ, _M_DQR, _M_DK, _M_DKR, _M_DV = 1920, 2176, 2432, 2688, 2944
_M_COLS = 3200
_T_CK, _T_CKR, _T_AKR, _T_AKRR, _T_ROWS = 0, 256, 512, 544, 576


def _cparams(sem):
    return pltpu.CompilerParams(dimension_semantics=sem, vmem_limit_bytes=V7X_VMEM_LIMIT)


def _rms(x):
    return x * lax.rsqrt(jnp.mean(x * x, axis=-1, keepdims=True) + RMS_EPS)


def _nt_dot(a, b):
    return lax.dot_general(a, b, (((1,), (1,)), ((), ())), preferred_element_type=F32)


def _head_of_lane(shape):
    return lax.broadcasted_iota(jnp.int32, shape, len(shape) - 1) // HEAD_DIM


def _proj_kernel(x_ref, g1_ref, wm_ref, wt_ref, gq_ref, wuq_ref, gkv_ref, wukt_ref, wuv_ref,
                 tab_ref, tabt_ref,
                 qa_ref, kat_ref, va_ref, qb_ref, kb_ref, vb_ref, qc_ref, kct_ref, vc_ref,
                 qd_ref, kd_ref, vd_ref):
    x = x_ref[...]
    h = (_rms(x) * g1_ref[...]).astype(BF16)
    p = jnp.dot(h, wm_ref[...], preferred_element_type=F32)
    pt = _nt_dot(wt_ref[...], h)
    tab = tab_ref[...]
    cos_a, sin_a, cos_c, sin_c, cos_d, sin_d = (tab[:, LANES * i:LANES * (i + 1)] for i in range(6))
    tabt = tabt_ref[...]
    cos_t, sin_t = tabt[0:A_ROPE], tabt[A_ROPE:2 * A_ROPE]

    scale_a = LOG2E * (A_NOPE + A_ROPE) ** -0.5
    latq = (_rms(p[:, _M_ACQ:_M_ACQ + A_Q_LORA]) * gq_ref[...]).astype(BF16)
    qa2 = jnp.dot(latq, wuq_ref[...], preferred_element_type=F32)
    for hd in range(N_HEADS):
        lo = A_PAD * hd
        blk = qa2[:, lo:lo + A_PAD] * cos_a + qa2[:, N_HEADS * A_PAD + lo:N_HEADS * A_PAD + lo + A_PAD] * sin_a
        qa_ref[:, lo:lo + A_PAD] = (blk * scale_a).astype(BF16)
    latkv = (_rms(p[:, _M_ACKV:_M_ACKV + A_KV_LORA]) * gkv_ref[...]).astype(BF16)
    va_ref[...] = jnp.dot(latkv, wuv_ref[...], preferred_element_type=F32).astype(BF16)
    knt = _nt_dot(wukt_ref[...], latkv)
    krt = (pt[_T_AKR:_T_AKR + A_ROPE] * cos_t + pt[_T_AKRR:_T_AKRR + A_ROPE] * sin_t).astype(BF16)
    tm = x.shape[0]
    for hd in range(N_HEADS):
        lo = A_PAD * hd
        kat_ref[lo:lo + A_NOPE, :] = knt[A_NOPE * hd:A_NOPE * (hd + 1)].astype(BF16)
        kat_ref[lo + A_NOPE:lo + A_NOPE + A_ROPE, :] = krt
        kat_ref[lo + A_NOPE + A_ROPE:lo + A_PAD, :] = jnp.zeros((A_PAD - A_NOPE - A_ROPE, tm), BF16)

    qb_ref[...] = (p[:, _M_BQ:_M_BQ + MIX_COLS] * (LOG2E * HEAD_DIM ** -0.5)).astype(BF16)
    kb_ref[...] = p[:, _M_BK:_M_BK + MIX_COLS].astype(BF16)
    vb_ref[...] = p[:, _M_BV:_M_BV + MIX_COLS].astype(BF16)

    scale_c = LOG2E * C_DIM ** -0.5
    for j in range(MIX_COLS // LANES):
        lo = LANES * j
        blk = p[:, _M_CQ + lo:_M_CQ + lo + LANES] * cos_c + p[:, _M_CQR + lo:_M_CQR + lo + LANES] * sin_c
        qc_ref[:, lo:lo + LANES] = (blk * scale_c).astype(BF16)
    reps = MIX_COLS // A_ROPE
    cos_ct = jnp.concatenate([cos_t] * reps, axis=0)
    sin_ct = jnp.concatenate([sin_t] * reps, axis=0)
    kct_ref[...] = (pt[_T_CK:_T_CK + MIX_COLS] * cos_ct + pt[_T_CKR:_T_CKR + MIX_COLS] * sin_ct).astype(BF16)
    vc_ref[...] = p[:, _M_CV:_M_CV + MIX_COLS].astype(BF16)

    for j in range(MIX_COLS // LANES):
        lo = LANES * j
        qblk = p[:, _M_DQ + lo:_M_DQ + lo + LANES] * cos_d + p[:, _M_DQR + lo:_M_DQR + lo + LANES] * sin_d
        qd_ref[:, lo:lo + LANES] = (qblk * (LOG2E * HEAD_DIM ** -0.5)).astype(BF16)
        kblk = p[:, _M_DK + lo:_M_DK + lo + LANES] * cos_d + p[:, _M_DKR + lo:_M_DKR + lo + LANES] * sin_d
        kd_ref[:, lo:lo + LANES] = kblk.astype(BF16)
    vd_ref[...] = p[:, _M_DV:_M_DV + MIX_COLS].astype(BF16)


def _proj_call(x2d, t_len, g1, wm, wt, gq, wuq, gkv, wukt, wuv, tab, tabt, tm):
    n = x2d.shape[0]
    nt = t_len // tm
    full = lambda a: pl.BlockSpec(a.shape, lambda i: (0,) * a.ndim)
    row = lambda w: pl.BlockSpec((tm, w), lambda i: (i, 0))
    col = lambda r: pl.BlockSpec((r, tm), lambda i: (0, i))
    tok = lambda w: jax.ShapeDtypeStruct((n, w), BF16)
    out_shape = (tok(N_HEADS * A_PAD), jax.ShapeDtypeStruct((N_HEADS * A_PAD, n), BF16), tok(MIX_COLS),
                 tok(MIX_COLS), tok(MIX_COLS), tok(MIX_COLS),
                 tok(MIX_COLS), jax.ShapeDtypeStruct((MIX_COLS, n), BF16), tok(MIX_COLS),
                 tok(MIX_COLS), tok(MIX_COLS), tok(MIX_COLS))
    out_specs = (row(N_HEADS * A_PAD), col(N_HEADS * A_PAD), row(MIX_COLS),
                 row(MIX_COLS), row(MIX_COLS), row(MIX_COLS),
                 row(MIX_COLS), col(MIX_COLS), row(MIX_COLS),
                 row(MIX_COLS), row(MIX_COLS), row(MIX_COLS))
    in_specs = [row(D_MODEL), full(g1), full(wm), full(wt), full(gq), full(wuq), full(gkv), full(wukt), full(wuv),
                pl.BlockSpec((tm, tab.shape[1]), lambda i: (i % nt, 0)),
                pl.BlockSpec((tabt.shape[0], tm), lambda i: (0, i % nt))]
    return pl.pallas_call(
        _proj_kernel, grid=(n // tm,), in_specs=in_specs, out_specs=out_specs, out_shape=out_shape,
        compiler_params=_cparams(("parallel",)), name="proj_in",
    )(x2d, g1, wm, wt, gq, wuq, gkv, wukt, wuv, tab, tabt)


def _row_total(l_lanes):
    return jnp.sum(l_lanes, axis=1, keepdims=True)


def _online_step(s, v, m_ref, l_ref, acc_ref, idx):
    m_prev = m_ref[idx]
    m_new = jnp.maximum(m_prev, jnp.max(s, axis=1, keepdims=True))
    alpha = jnp.exp2(m_prev - m_new)
    p = jnp.exp2(s - m_new)
    part = p[:, 0:LANES]
    for c in range(1, s.shape[1] // LANES):
        part = part + p[:, LANES * c:LANES * (c + 1)]
    l_ref[idx] = alpha * l_ref[idx] + part
    acc_ref[idx] = alpha * acc_ref[idx] + jnp.dot(p.astype(BF16), v, preferred_element_type=F32)
    m_ref[idx] = m_new


def _mla_kernel(q_ref, kt_ref, v_ref, o_ref, m_ref, l_ref, acc_ref):
    j = pl.program_id(2)

    @pl.when(j == 0)
    def _():
        m_ref[...] = jnp.full(m_ref.shape, -jnp.inf, F32)
        l_ref[...] = jnp.zeros(l_ref.shape, F32)
        acc_ref[...] = jnp.zeros(acc_ref.shape, F32)

    v = v_ref[...]
    for hd in range(N_HEADS):
        lo = A_PAD * hd
        s = jnp.dot(q_ref[:, lo:lo + A_PAD], kt_ref[lo:lo + A_PAD, :], preferred_element_type=F32)
        _online_step(s, v, m_ref, l_ref, acc_ref, hd)

    @pl.when(j == pl.num_programs(2) - 1)
    def _():
        head = _head_of_lane(o_ref.shape)
        out = jnp.zeros(o_ref.shape, F32)
        for hd in range(N_HEADS):
            out = out + jnp.where(head == hd, acc_ref[hd] * (1.0 / _row_total(l_ref[hd])), 0.0)
        o_ref[...] = out.astype(o_ref.dtype)


def _mla_call(q, kt, v, bsz, t_len, tq, tk):
    n = q.shape[0]
    nq, nk = t_len // tq, t_len // tk
    return pl.pallas_call(
        _mla_kernel, grid=(bsz, nq, nk),
        in_specs=[pl.BlockSpec((tq, N_HEADS * A_PAD), lambda b, i, j: (b * nq + i, 0)),
                  pl.BlockSpec((N_HEADS * A_PAD, tk), lambda b, i, j: (0, b * nk + j)),
                  pl.BlockSpec((tk, MIX_COLS), lambda b, i, j: (b * nk + j, 0))],
        out_specs=pl.BlockSpec((tq, MIX_COLS), lambda b, i, j: (b * nq + i, 0)),
        out_shape=jax.ShapeDtypeStruct((n, MIX_COLS), BF16),
        scratch_shapes=[pltpu.VMEM((N_HEADS, tq, 1), F32), pltpu.VMEM((N_HEADS, tq, LANES), F32),
                        pltpu.VMEM((N_HEADS, tq, MIX_COLS), F32)],
        compiler_params=_cparams(("parallel", "parallel", "arbitrary")), name="attn_mla",
    )(q, kt, v)


def _diff_kernel(lam_init, q_ref, kt_ref, v_ref, lam_ref, g_ref, o_ref, qm_ref, m_ref, l_ref, acc_ref):
    j = pl.program_id(2)
    n_maps = 2 * N_HEADS

    @pl.when(j == 0)
    def _():
        m_ref[...] = jnp.full(m_ref.shape, -jnp.inf, F32)
        l_ref[...] = jnp.zeros(l_ref.shape, F32)
        acc_ref[...] = jnp.zeros(acc_ref.shape, F32)
        q = q_ref[...]
        group = lax.broadcasted_iota(jnp.int32, q.shape, 1) // C_DIM
        for mi in range(n_maps):
            qm_ref[mi] = jnp.where(group == mi, q, jnp.zeros_like(q))

    v = v_ref[...]
    kt = kt_ref[...]
    for mi in range(n_maps):
        s = jnp.dot(qm_ref[mi], kt, preferred_element_type=F32)
        _online_step(s, v, m_ref, l_ref, acc_ref, mi)

    @pl.when(j == pl.num_programs(2) - 1)
    def _():
        lam = lam_ref[...]
        head = _head_of_lane(o_ref.shape)
        o = jnp.zeros(o_ref.shape, F32)
        for hd in range(N_HEADS):
            oh = (acc_ref[2 * hd] * (1.0 / _row_total(l_ref[2 * hd]))
                  - lam * (acc_ref[2 * hd + 1] * (1.0 / _row_total(l_ref[2 * hd + 1]))))
            o = o + jnp.where(head == hd, oh, 0.0)
        o2 = o * o
        inv = jnp.zeros(o_ref.shape, F32)
        for hd in range(N_HEADS):
            ms = jnp.sum(jnp.where(head == hd, o2, 0.0), axis=1, keepdims=True) * (1.0 / HEAD_DIM)
            inv = inv + jnp.where(head == hd, lax.rsqrt(ms + RMS_EPS), 0.0)
        o_ref[...] = ((o * inv * g_ref[...]) * (1.0 - lam_init)).astype(o_ref.dtype)


def _diff_call(q, kt, v, lam, g_tiled, lam_init, bsz, t_len, tq, tk):
    n = q.shape[0]
    nq, nk = t_len // tq, t_len // tk
    n_maps = 2 * N_HEADS
    return pl.pallas_call(
        functools.partial(_diff_kernel, lam_init), grid=(bsz, nq, nk),
        in_specs=[pl.BlockSpec((tq, MIX_COLS), lambda b, i, j: (b * nq + i, 0)),
                  pl.BlockSpec((MIX_COLS, tk), lambda b, i, j: (0, b * nk + j)),
                  pl.BlockSpec((tk, MIX_COLS), lambda b, i, j: (b * nk + j, 0)),
                  pl.BlockSpec((1, 1), lambda b, i, j: (0, 0)),
                  pl.BlockSpec((1, MIX_COLS), lambda b, i, j: (0, 0))],
        out_specs=pl.BlockSpec((tq, MIX_COLS), lambda b, i, j: (b * nq + i, 0)),
        out_shape=jax.ShapeDtypeStruct((n, MIX_COLS), BF16),
        scratch_shapes=[pltpu.VMEM((n_maps, tq, MIX_COLS), BF16),
                        pltpu.VMEM((n_maps, tq, 1), F32), pltpu.VMEM((n_maps, tq, LANES), F32),
                        pltpu.VMEM((n_maps, tq, MIX_COLS), F32)],
        compiler_params=_cparams(("parallel", "parallel", "arbitrary")), name="attn_diff",
    )(q, kt, v, lam, g_tiled)


def _window_heads(q, kw, vw, add, mult, o_ref):
    head = _head_of_lane(q.shape)
    out = jnp.zeros(q.shape, F32)
    for hd in range(N_HEADS):
        qh = jnp.where(head == hd, q, jnp.zeros_like(q))
        s = _nt_dot(qh, kw)
        if add is not None:
            s = s + add(hd)
        if mult is not None:
            s = jnp.where(mult > 0.0, s, NEG_BIG)
        m = jnp.max(s, axis=1, keepdims=True)
        p = jnp.exp2(s - m)
        if mult is not None:
            p = p * mult
        l = jnp.sum(p, axis=1, keepdims=True)
        o = jnp.dot(p.astype(BF16), vw, preferred_element_type=F32)
        out = out + jnp.where(head == hd, o * (1.0 / l), 0.0)
    o_ref[...] = out.astype(o_ref.dtype)


def _nbr_kernel(nb, q_ref, k_ref, v_ref, bias_ref, o_ref):
    i = pl.program_id(1)
    start = pl.multiple_of(jnp.clip(i - 2, 0, nb - B_WIN // QBLK) * QBLK, QBLK)
    kw = k_ref[pl.ds(start, B_WIN), :]
    vw = v_ref[pl.ds(start, B_WIN), :]
    _window_heads(q_ref[...], kw, vw, lambda hd: bias_ref[0, hd], None, o_ref)


def _nbr_block_type(i, nb):
    return jnp.where(i < 2, i, jnp.where(i > nb - 3, i - nb + 5, 2))


def _nbr_call(q, k, v, bias, bsz, t_len):
    n = q.shape[0]
    nb = t_len // QBLK
    return pl.pallas_call(
        functools.partial(_nbr_kernel, nb), grid=(bsz, nb),
        in_specs=[pl.BlockSpec((QBLK, MIX_COLS), lambda b, i: (b * nb + i, 0)),
                  pl.BlockSpec((t_len, MIX_COLS), lambda b, i: (b, 0)),
                  pl.BlockSpec((t_len, MIX_COLS), lambda b, i: (b, 0)),
                  pl.BlockSpec((1, N_HEADS, QBLK, B_WIN), lambda b, i: (_nbr_block_type(i, nb), 0, 0, 0))],
        out_specs=pl.BlockSpec((QBLK, MIX_COLS), lambda b, i: (b * nb + i, 0)),
        out_shape=jax.ShapeDtypeStruct((n, MIX_COLS), BF16),
        compiler_params=_cparams(("parallel", "arbitrary")), name="attn_nbr",
    )(q, k, v, bias)


def _nbr_bias_tables(rpb, t_len):
    rows = t_len // GRID_W
    nb = t_len // QBLK
    kr = min(NA_ROWS, rows)
    reps = (0, 1, 2, nb - 2, nb - 1)
    n_dr, n_dc = 2 * NA_ROWS - 1, 2 * NA_COLS - 1
    q_rows, w_rows = QBLK // GRID_W, B_WIN // GRID_W
    qc = np.arange(GRID_W)[:, None]
    kc = np.arange(GRID_W)[None, :]
    cs = np.clip(qc - NA_COLS // 2, 0, GRID_W - NA_COLS)
    col_ok = (kc >= cs) & (kc < cs + NA_COLS)
    sel_c = ((kc - qc + NA_COLS - 1)[None] == np.arange(n_dc)[:, None, None]) & col_ok[None]
    sel_r = np.zeros((len(reps), q_rows, w_rows, n_dr), np.float32)
    row_ok = np.zeros((len(reps), q_rows, w_rows), bool)
    for ti, b in enumerate(reps):
        start_row = int(np.clip(b - 2, 0, nb - B_WIN // QBLK)) * q_rows
        for qr in range(q_rows):
            r = b * q_rows + qr
            rs = int(np.clip(r - kr // 2, 0, rows - kr))
            for wr in range(w_rows):
                key_r = start_row + wr
                if rs <= key_r < rs + kr:
                    row_ok[ti, qr, wr] = True
                    sel_r[ti, qr, wr, key_r - r + NA_ROWS - 1] = 1.0
    hp = lax.Precision.HIGHEST
    by_col = jnp.einsum("hrd,dqk->hrqk", rpb.astype(F32), jnp.asarray(sel_c, F32), precision=hp)
    dense = jnp.einsum("tawr,hrqk->thaqwk", jnp.asarray(sel_r), by_col, precision=hp) * LOG2E
    ok = row_ok[:, None, :, None, :, None] & col_ok[None, None, None, :, None, :]
    return jnp.where(ok, dense, NEG_BIG).reshape(len(reps), N_HEADS, QBLK, B_WIN)


def _dil_kernel(t_len, tq, q_ref, k_ref, v_ref, o_ref):
    i = pl.program_id(1)
    w = tq + 2 * D_REACH
    t0 = i * tq
    start = pl.multiple_of(jnp.clip(t0 - D_REACH, 0, t_len - w), tq)
    kw = k_ref[pl.ds(start, w), :]
    vw = v_ref[pl.ds(start, w), :]
    d = (start - t0) + lax.broadcasted_iota(jnp.int32, (tq, w), 1) - lax.broadcasted_iota(jnp.int32, (tq, w), 0)
    ad = jnp.abs(d)
    mult = ((ad <= 64).astype(F32)
            + (((d & 3) == 0) & (ad <= 256)).astype(F32)
            + (((d & 15) == 0) & (ad <= D_REACH)).astype(F32))
    _window_heads(q_ref[...], kw, vw, None, mult, o_ref)


def _dil_call(q, k, v, bsz, t_len, tq):
    n = q.shape[0]
    nq = t_len // tq
    return pl.pallas_call(
        functools.partial(_dil_kernel, t_len, tq), grid=(bsz, nq),
        in_specs=[pl.BlockSpec((tq, MIX_COLS), lambda b, i: (b * nq + i, 0)),
                  pl.BlockSpec((t_len, MIX_COLS), lambda b, i: (b, 0)),
                  pl.BlockSpec((t_len, MIX_COLS), lambda b, i: (b, 0))],
        out_specs=pl.BlockSpec((tq, MIX_COLS), lambda b, i: (b * nq + i, 0)),
        out_shape=jax.ShapeDtypeStruct((n, MIX_COLS), BF16),
        compiler_params=_cparams(("parallel", "arbitrary")), name="attn_dil",
    )(q, k, v)


def _out_kernel(oa_ref, ob_ref, oc_ref, od_ref, x_ref, wo_ref, g2_ref, wr_ref, x2_ref, hn_ref, aff_ref):
    acc = x_ref[...]
    for mi, o_ref in enumerate((oa_ref, ob_ref, oc_ref, od_ref)):
        acc = acc + jnp.dot(o_ref[...], wo_ref[MIX_COLS * mi:MIX_COLS * (mi + 1), :], preferred_element_type=F32)
    x2_ref[...] = acc
    hn = _rms(acc) * g2_ref[...]
    hn_ref[...] = hn.astype(BF16)
    logits = jnp.dot(hn, wr_ref[...], preferred_element_type=F32, precision=lax.Precision.HIGHEST)
    lane = lax.broadcasted_iota(jnp.int32, logits.shape, 1)
    logits = jnp.where(lane < N_EXPERTS, logits, -jnp.inf)
    m = jnp.max(logits, axis=1, keepdims=True)
    e = jnp.exp(logits - m)
    aff_ref[...] = e / jnp.sum(e, axis=1, keepdims=True)


def _out_call(oa, ob, oc, od, x2d, wo, g2, wr, tm):
    n = x2d.shape[0]
    full = lambda a: pl.BlockSpec(a.shape, lambda i: (0,) * a.ndim)
    row = lambda w: pl.BlockSpec((tm, w), lambda i: (i, 0))
    return pl.pallas_call(
        _out_kernel, grid=(n // tm,),
        in_specs=[row(MIX_COLS)] * 4 + [row(D_MODEL), full(wo), full(g2), full(wr)],
        out_specs=(row(D_MODEL), row(D_MODEL), row(LANES)),
        out_shape=(jax.ShapeDtypeStruct((n, D_MODEL), F32), jax.ShapeDtypeStruct((n, D_MODEL), BF16),
                   jax.ShapeDtypeStruct((n, LANES), F32)),
        compiler_params=_cparams(("parallel",)), name="proj_out",
    )(oa, ob, oc, od, x2d, wo, g2, wr)


FLAG_FIRST, FLAG_LAST, FLAG_SUB = 1, 2, 4
Y_BLK = 128
GATHER_FANIN = 4
COMBINE_FANIN = 8


def _count_le(sorted_vals, queries):
    return jnp.sum((sorted_vals[None, :] <= queries[:, None]).astype(jnp.int32), axis=1)


def _step_lists(cnt, fanin, n_steps):
    steps = jnp.maximum((cnt + fanin - 1) // fanin, 1)
    ends = jnp.cumsum(steps)
    w = jnp.arange(n_steps, dtype=jnp.int32)
    grp = jnp.minimum(_count_le(ends, w), cnt.shape[0] - 1)
    per_grp = jnp.stack([ends - steps, steps, cnt], axis=1)[grp]
    rank = w - per_grp[:, 0]
    valid = w < ends[-1]
    item0 = rank * fanin
    flags = (jnp.where(valid & (rank == 0), FLAG_FIRST, 0)
             | jnp.where(valid & (rank == per_grp[:, 1] - 1), FLAG_LAST, 0))
    for k in range(fanin):
        flags = flags | jnp.where(valid & (item0 + k < per_grp[:, 2]), FLAG_SUB << k, 0)
    return grp, item0, flags.astype(jnp.int32)


def _route_tables(aff, n, cap, ts, tc, tt):
    gate, idx = lax.top_k(aff[:, :N_EXPERTS].T, cap)
    idx, gate = lax.sort((idx, gate), dimension=1, num_keys=1)
    idx = idx.astype(jnp.int32)

    tiles_per_e = cap // ts
    n_tiles = N_EXPERTS * tiles_per_e
    n_chunks = n // tc
    c0 = (idx[:, ::ts] // tc).reshape(n_tiles)
    c1 = (idx[:, ts - 1::ts] // tc).reshape(n_tiles)
    n_gs = (N_EXPERTS * n_chunks + n_tiles) // GATHER_FANIN + n_tiles
    g_tile, g_item0, g_flags = _step_lists(c1 - c0 + 1, GATHER_FANIN, n_gs)
    g_first = c0[g_tile] + g_item0
    g_chunks = jnp.concatenate([jnp.minimum(g_first + k, n_chunks - 1) for k in range(GATHER_FANIN)])

    n_tt = n // tt
    blk_per_e = cap // Y_BLK
    bounds = jnp.arange(n_tt + 1, dtype=jnp.int32) * tt
    pos = jnp.sum((idx[:, None, :] < bounds[None, :, None]).astype(jnp.int32), axis=2)
    lo, hi = pos[:, :-1], pos[:, 1:]
    b0 = jnp.minimum(lo // Y_BLK, blk_per_e - 1)
    b1 = jnp.maximum((hi - 1) // Y_BLK, b0)
    pair_cnt = jnp.where(hi > lo, b1 - b0 + 1, 0).T.reshape(-1)
    pair_b0 = (b0 + (jnp.arange(N_EXPERTS, dtype=jnp.int32) * blk_per_e)[:, None]).T.reshape(-1)
    n_items = N_EXPERTS * blk_per_e + N_EXPERTS * n_tt
    item_ends = jnp.cumsum(pair_cnt)
    it = jnp.arange(n_items, dtype=jnp.int32)
    pair = jnp.minimum(_count_le(item_ends, it), pair_cnt.shape[0] - 1)
    item_blk = jnp.minimum(pair_b0[pair] + it - (item_ends - pair_cnt)[pair], N_EXPERTS * blk_per_e - 1)
    tile_cnt = jnp.sum(pair_cnt.reshape(n_tt, N_EXPERTS), axis=1)
    tile_item0 = jnp.cumsum(tile_cnt) - tile_cnt
    n_cs = n_items // COMBINE_FANIN + n_tt
    c_tile, c_item0, c_flags = _step_lists(tile_cnt, COMBINE_FANIN, n_cs)
    c_first = tile_item0[c_tile] + c_item0
    c_blks = jnp.concatenate([item_blk[jnp.minimum(c_first + k, n_items - 1)] for k in range(COMBINE_FANIN)])
    return idx, gate, (g_tile, g_chunks, g_flags), (c_tile, c_blks, c_flags)


def _ffn_kernel(tc, tile_ref, chunk_ref, flag_ref, tok_ref, gate_ref, *refs):
    hn_refs = refs[:GATHER_FANIN]
    wg_ref, wu_ref, wd_ref, y_ref, xacc = refs[GATHER_FANIN:]
    w = pl.program_id(0)
    n_steps = pl.num_programs(0)
    flags = flag_ref[w]

    @pl.when((flags & FLAG_FIRST) != 0)
    def _():
        xacc[...] = jnp.zeros(xacc.shape, F32)

    for k in range(GATHER_FANIN):
        @pl.when((flags & (FLAG_SUB << k)) != 0)
        def _(k=k):
            ts = tok_ref.shape[0]
            token = chunk_ref[k * n_steps + w] * tc + lax.broadcasted_iota(jnp.int32, (ts, tc), 1)
            onehot = jnp.where(tok_ref[...] == token, 1.0, 0.0).astype(BF16)
            xacc[...] += jnp.dot(onehot, hn_refs[k][...], preferred_element_type=F32)

    @pl.when((flags & FLAG_LAST) != 0)
    def _():
        xg = xacc[...].astype(BF16)
        a = jnp.dot(xg, wg_ref[0, 0], preferred_element_type=F32)
        b = jnp.dot(xg, wu_ref[0, 0], preferred_element_type=F32)
        hid = (a * jax.nn.sigmoid(a) * b).astype(BF16)
        y_ref[...] = (jnp.dot(hid, wd_ref[0, 0], preferred_element_type=F32) * gate_ref[...]).astype(BF16)


def _ffn_call(steps, idx, gate, hn, w_gate, w_up, w_down, layer, ts, tc):
    n_exp, cap = idx.shape
    tiles_per_e = cap // ts
    tile, chunks, flags = steps
    n_steps = tile.shape[0]
    tok = idx.reshape(n_exp * cap, 1)
    gate2 = gate.reshape(n_exp * cap, 1)
    slot = lambda wd: pl.BlockSpec((ts, wd), lambda w, tile, chunks, flags: (tile[w], 0))
    chunk = lambda k: pl.BlockSpec((tc, D_MODEL), lambda w, tile, chunks, flags: (chunks[k * n_steps + w], 0))
    wspec = pl.BlockSpec((1, 1, D_MODEL, D_MODEL), lambda w, tile, chunks, flags: (layer, tile[w] // tiles_per_e, 0, 0))
    grid_spec = pltpu.PrefetchScalarGridSpec(
        num_scalar_prefetch=3, grid=(n_steps,),
        in_specs=[slot(1), slot(1)] + [chunk(k) for k in range(GATHER_FANIN)] + [wspec, wspec, wspec],
        out_specs=slot(D_MODEL),
        scratch_shapes=[pltpu.VMEM((ts, D_MODEL), F32)])
    return pl.pallas_call(
        functools.partial(_ffn_kernel, tc), grid_spec=grid_spec,
        out_shape=jax.ShapeDtypeStruct((n_exp * cap, D_MODEL), BF16),
        compiler_params=_cparams(("arbitrary",)), name="expert_ffn",
    )(tile, chunks, flags, tok, gate2, *([hn] * GATHER_FANIN), w_gate, w_up, w_down)


def _combine_kernel(tt, final, tile_ref, blk_ref, flag_ref, *refs):
    tok_refs = refs[:COMBINE_FANIN]
    y_refs = refs[COMBINE_FANIN:2 * COMBINE_FANIN]
    x_ref, g_ref, o_ref = refs[2 * COMBINE_FANIN:]
    w = pl.program_id(0)
    flags = flag_ref[w]

    @pl.when((flags & FLAG_FIRST) != 0)
    def _():
        o_ref[...] = x_ref[...]

    @pl.when((flags & FLAG_SUB) != 0)
    def _():
        token = tile_ref[w] * tt + lax.broadcasted_iota(jnp.int32, (tt, Y_BLK), 0)
        hots = []
        for k in range(COMBINE_FANIN):
            live = (flags & (FLAG_SUB << k)) != 0
            tok = jnp.where(live, tok_refs[k][0], -1)
            hots.append(jnp.where(token == tok, 1.0, 0.0).astype(BF16))
        onehot = jnp.concatenate(hots, axis=1)
        ycat = jnp.concatenate([y_refs[k][...] for k in range(COMBINE_FANIN)], axis=0)
        o_ref[...] += jnp.dot(onehot, ycat, preferred_element_type=F32)

    if final:
        @pl.when((flags & FLAG_LAST) != 0)
        def _():
            o_ref[...] = _rms(o_ref[...]) * g_ref[...]


def _combine_call(steps, idx, y, x2, g_final, tt, final):
    n = x2.shape[0]
    tile, blks, flags = steps
    n_steps = tile.shape[0]
    tok = idx.reshape(-1, 1, Y_BLK)
    tspec = lambda k: pl.BlockSpec((1, 1, Y_BLK), lambda w, tile, blks, flags: (blks[k * n_steps + w], 0, 0))
    yspec = lambda k: pl.BlockSpec((Y_BLK, D_MODEL), lambda w, tile, blks, flags: (blks[k * n_steps + w], 0))
    xspec = pl.BlockSpec((tt, D_MODEL), lambda w, tile, blks, flags: (tile[w], 0))
    fan = range(COMBINE_FANIN)
    grid_spec = pltpu.PrefetchScalarGridSpec(
        num_scalar_prefetch=3, grid=(n_steps,),
        in_specs=[tspec(k) for k in fan] + [yspec(k) for k in fan]
        + [xspec, pl.BlockSpec((1, D_MODEL), lambda w, tile, blks, flags: (0, 0))],
        out_specs=xspec)
    return pl.pallas_call(
        functools.partial(_combine_kernel, tt, final), grid_spec=grid_spec,
        out_shape=jax.ShapeDtypeStruct((n, D_MODEL), F32),
        compiler_params=_cparams(("arbitrary",)), name="expert_combine",
    )(tile, blks, flags, *([tok] * COMBINE_FANIN), *([y] * COMBINE_FANIN), x2, g_final)


def _rot_cols(w, d):
    k, c = w.shape
    half = d // 2
    wg = w.reshape(k, c // d, 2, half)
    return jnp.concatenate([-wg[:, :, 1], wg[:, :, 0]], axis=2).reshape(k, c)


def _rope_tables(t_len):
    pos = jnp.arange(t_len, dtype=F32)

    def cs(d):
        half = d // 2
        inv = ROPE_THETA ** (-jnp.arange(half, dtype=F32) / half)
        ang = pos[:, None] * inv[None, :]
        return (jnp.concatenate([jnp.cos(ang)] * 2, axis=1), jnp.concatenate([jnp.sin(ang)] * 2, axis=1))

    c32, s32 = cs(A_ROPE)
    c64, s64 = cs(HEAD_DIM)
    ones = jnp.ones((t_len, A_NOPE), F32)
    zeros = jnp.zeros((t_len, A_NOPE), F32)
    pad = jnp.zeros((t_len, A_PAD - A_NOPE - A_ROPE), F32)
    tab = jnp.concatenate([ones, c32, pad, zeros, s32, pad,
                           jnp.tile(c32, (1, LANES // A_ROPE)), jnp.tile(s32, (1, LANES // A_ROPE)),
                           jnp.tile(c64, (1, LANES // HEAD_DIM)), jnp.tile(s64, (1, LANES // HEAD_DIM))], axis=1)
    tabt = jnp.concatenate([c32.T, s32.T], axis=0)
    return tab, tabt


def _layer_weights(l, w_in, a_w_uq, a_w_ukv, w_out, w_router):
    pts = np.cumsum(PROJ_SIZES)[:-1]
    (a_cq, a_ckv, a_kr, b_q, b_k, b_v, c_q, c_k, c_v, d_q, d_k, d_v) = jnp.split(w_in[l], pts, axis=1)
    wm = jnp.concatenate([a_cq, a_ckv, b_q, b_k, b_v, c_q, _rot_cols(c_q, C_DIM), c_v,
                          d_q, _rot_cols(d_q, HEAD_DIM), d_k, _rot_cols(d_k, HEAD_DIM), d_v], axis=1).astype(BF16)
    wt = jnp.concatenate([c_k, _rot_cols(c_k, C_DIM), a_kr, _rot_cols(a_kr, A_ROPE)], axis=1).T.astype(BF16)
    uq = a_w_uq[l].reshape(A_Q_LORA, N_HEADS, A_NOPE + A_ROPE)
    zpad = jnp.zeros((A_Q_LORA, N_HEADS, A_PAD - A_NOPE - A_ROPE), F32)
    uq_rope = uq[:, :, A_NOPE:]
    uq_rot = _rot_cols(uq_rope.reshape(A_Q_LORA, N_HEADS * A_ROPE), A_ROPE).reshape(A_Q_LORA, N_HEADS, A_ROPE)
    plain = jnp.concatenate([uq, zpad], axis=2).reshape(A_Q_LORA, N_HEADS * A_PAD)
    rot = jnp.concatenate([jnp.zeros_like(uq[:, :, :A_NOPE]), uq_rot, zpad], axis=2).reshape(A_Q_LORA, N_HEADS * A_PAD)
    wuq = jnp.concatenate([plain, rot], axis=1).astype(BF16)
    ukv = a_w_ukv[l].reshape(A_KV_LORA, N_HEADS, 2 * HEAD_DIM)
    wukt = ukv[:, :, :A_NOPE].reshape(A_KV_LORA, MIX_COLS).T.astype(BF16)
    wuv = ukv[:, :, A_NOPE:].reshape(A_KV_LORA, MIX_COLS).astype(BF16)
    wo = w_out[l].astype(BF16)
    wr = jnp.concatenate([w_router[l], jnp.zeros((D_MODEL, LANES - N_EXPERTS), F32)], axis=1)
    return wm, wt, wuq, wukt, wuv, wo, wr


def _trunk(x, norm1_g, w_in, a_qnorm_g, a_w_uq, a_kvnorm_g, a_w_ukv, b_rpb, c_lambda, c_subln_g, w_out,
           norm2_g, w_router, w_gate, w_up, w_down, final_g):
    bsz, t_len, _ = x.shape
    n = bsz * t_len
    tm = 512
    tq = 512
    tk_mla, tk_diff = min(4096, t_len), 2048
    tq_dil = 256
    ts, tc, tt = 256, 1024, 512
    depth = w_in.shape[0]
    cap = CAP_FACTOR * n // N_EXPERTS
    assert t_len % tk_mla == 0 and t_len % tk_diff == 0 and t_len >= tq_dil + 2 * D_REACH and t_len // QBLK >= 5
    assert cap % ts == 0 and n % tc == 0 and n % tt == 0
    tab, tabt = _rope_tables(t_len)
    x2d = x.reshape(n, D_MODEL)
    g_final = final_g.reshape(1, D_MODEL)
    for l in range(depth):
        lam_init = 0.8 - 0.6 * math.exp(-0.3 * l)
        wm, wt, wuq, wukt, wuv, wo, wr = _layer_weights(l, w_in, a_w_uq, a_w_ukv, w_out, w_router)
        row = lambda g: g[l].reshape(1, -1)
        (qa, kat, va, qb, kb, vb, qc, kct, vc, qd, kd, vd) = _proj_call(
            x2d, t_len, row(norm1_g), wm, wt, row(a_qnorm_g), wuq, row(a_kvnorm_g), wukt, wuv, tab, tabt, tm)
        oa = _mla_call(qa, kat, va, bsz, t_len, tq, tk_mla)
        ob = _nbr_call(qb, kb, vb, _nbr_bias_tables(b_rpb[l], t_len), bsz, t_len)
        lp = c_lambda[l].astype(F32)
        lam = (jnp.exp(jnp.sum(lp[0] * lp[1])) - jnp.exp(jnp.sum(lp[2] * lp[3])) + lam_init).reshape(1, 1)
        g_sub = jnp.tile(c_subln_g[l], N_HEADS).reshape(1, MIX_COLS)
        oc = _diff_call(qc, kct, vc, lam, g_sub, lam_init, bsz, t_len, tq, tk_diff)
        od = _dil_call(qd, kd, vd, bsz, t_len, tq_dil)
        x2, hn, aff = _out_call(oa, ob, oc, od, x2d, wo, row(norm2_g), wr, tm)
        idx, gate, g_items, c_items = _route_tables(aff, n, cap, ts, tc, tt)
        y = _ffn_call(g_items, idx, gate, hn, w_gate, w_up, w_down, l, ts, tc)
        x2d = _combine_call(c_items, idx, y, x2, g_final, tt, l == depth - 1)
    return x2d.reshape(bsz, t_len, D_MODEL)


def kernel(x_prompt, x_sample, norm1_g, w_in, a_qnorm_g, a_w_uq, a_kvnorm_g, a_w_ukv, b_rpb, c_lambda, c_subln_g,
           w_out, norm2_g, w_router, w_gate, w_up, w_down, final_g):
    params = (norm1_g, w_in, a_qnorm_g, a_w_uq, a_kvnorm_g, a_w_ukv, b_rpb, c_lambda, c_subln_g, w_out,
              norm2_g, w_router, w_gate.astype(BF16), w_up.astype(BF16), w_down.astype(BF16), final_g)
    return (_trunk(x_prompt, *params), _trunk(x_sample, *params))
```

```python
import functools
import math

import numpy as np
import jax
import jax.numpy as jnp
from jax import lax
from jax.experimental import pallas as pl
from jax.experimental.pallas import tpu as pltpu

F32 = jnp.float32
BF16 = jnp.bfloat16

D_MODEL = 1024
N_HEADS = 4
HEAD_DIM = 64
MIX_COLS = N_HEADS * HEAD_DIM
A_NOPE = 64
A_ROPE = 32
A_PAD = 128
A_Q_LORA = 256
A_KV_LORA = 128
C_DIM = 32
GRID_W = 64
NA_ROWS = 8
NA_COLS = 16
QBLK = 256
B_WIN = 3 * QBLK
D_REACH = 1024
N_EXPERTS = 16
CAP_FACTOR = 2
ROPE_THETA = 10000.0
RMS_EPS = 1e-6
NEG_BIG = -1e30
LOG2E = math.log2(math.e)
LANES = 128
V7X_VMEM_LIMIT = 56 * 1024 * 1024

PROJ_SIZES = (256, 128, 32, 256, 256, 256, 256, 256, 256, 256, 256, 256)

_M_ACQ, _M_ACKV, _M_BQ, _M_BK, _M_BV = 0, 256, 384, 640, 896
_M_CQ, _M_CQR, _M_CV = 1152, 1408, 1664
_M_DQ, _M_DQR, _M_DK, _M_DKR, _M_DV = 1920, 2176, 2432, 2688, 2944
_M_COLS = 3200
_T_CK, _T_CKR, _T_AKR, _T_AKRR, _T_ROWS = 0, 256, 512, 544, 576


def _cparams(sem):
    return pltpu.CompilerParams(dimension_semantics=sem, vmem_limit_bytes=V7X_VMEM_LIMIT)


def _rms(x):
    return x * lax.rsqrt(jnp.mean(x * x, axis=-1, keepdims=True) + RMS_EPS)


def _nt_dot(a, b):
    return lax.dot_general(a, b, (((1,), (1,)), ((), ())), preferred_element_type=F32)


def _head_of_lane(shape):
    return lax.broadcasted_iota(jnp.int32, shape, len(shape) - 1) // HEAD_DIM


def _proj_kernel(x_ref, g1_ref, wm_ref, wt_ref, gq_ref, wuq_ref, gkv_ref, wukt_ref, wuv_ref,
                 tab_ref, tabt_ref,
                 qa_ref, kat_ref, va_ref, qb_ref, kb_ref, vb_ref, qc_ref, kct_ref, vc_ref,
                 qd_ref, kd_ref, vd_ref):
    x = x_ref[...]
    h = (_rms(x) * g1_ref[...]).astype(BF16)
    p = jnp.dot(h, wm_ref[...], preferred_element_type=F32)
    pt = _nt_dot(wt_ref[...], h)
    tab = tab_ref[...]
    cos_a, sin_a, cos_c, sin_c, cos_d, sin_d = (tab[:, LANES * i:LANES * (i + 1)] for i in range(6))
    tabt = tabt_ref[...]
    cos_t, sin_t = tabt[0:A_ROPE], tabt[A_ROPE:2 * A_ROPE]

    scale_a = LOG2E * (A_NOPE + A_ROPE) ** -0.5
    latq = (_rms(p[:, _M_ACQ:_M_ACQ + A_Q_LORA]) * gq_ref[...]).astype(BF16)
    qa2 = jnp.dot(latq, wuq_ref[...], preferred_element_type=F32)
    for hd in range(N_HEADS):
        lo = A_PAD * hd
        blk = qa2[:, lo:lo + A_PAD] * cos_a + qa2[:, N_HEADS * A_PAD + lo:N_HEADS * A_PAD + lo + A_PAD] * sin_a
        qa_ref[:, lo:lo + A_PAD] = (blk * scale_a).astype(BF16)
    latkv = (_rms(p[:, _M_ACKV:_M_ACKV + A_KV_LORA]) * gkv_ref[...]).astype(BF16)
    va_ref[...] = jnp.dot(latkv, wuv_ref[...], preferred_element_type=F32).astype(BF16)
    knt = _nt_dot(wukt_ref[...], latkv)
    krt = (pt[_T_AKR:_T_AKR + A_ROPE] * cos_t + pt[_T_AKRR:_T_AKRR + A_ROPE] * sin_t).astype(BF16)
    tm = x.shape[0]
    for hd in range(N_HEADS):
        lo = A_PAD * hd
        kat_ref[lo:lo + A_NOPE, :] = knt[A_NOPE * hd:A_NOPE * (hd + 1)].astype(BF16)
        kat_ref[lo + A_NOPE:lo + A_NOPE + A_ROPE, :] = krt
        kat_ref[lo + A_NOPE + A_ROPE:lo + A_PAD, :] = jnp.zeros((A_PAD - A_NOPE - A_ROPE, tm), BF16)

    qb_ref[...] = (p[:, _M_BQ:_M_BQ + MIX_COLS] * (LOG2E * HEAD_DIM ** -0.5)).astype(BF16)
    kb_ref[...] = p[:, _M_BK:_M_BK + MIX_COLS].astype(BF16)
    vb_ref[...] = p[:, _M_BV:_M_BV + MIX_COLS].astype(BF16)

    scale_c = LOG2E * C_DIM ** -0.5
    for j in range(MIX_COLS // LANES):
        lo = LANES * j
        blk = p[:, _M_CQ + lo:_M_CQ + lo + LANES] * cos_c + p[:, _M_CQR + lo:_M_CQR + lo + LANES] * sin_c
        qc_ref[:, lo:lo + LANES] = (blk * scale_c).astype(BF16)
    reps = MIX_COLS // A_ROPE
    cos_ct = jnp.concatenate([cos_t] * reps, axis=0)
    sin_ct = jnp.concatenate([sin_t] * reps, axis=0)
    kct_ref[...] = (pt[_T_CK:_T_CK + MIX_COLS] * cos_ct + pt[_T_CKR:_T_CKR + MIX_COLS] * sin_ct).astype(BF16)
    vc_ref[...] = p[:, _M_CV:_M_CV + MIX_COLS].astype(BF16)

    for j in range(MIX_COLS // LANES):
        lo = LANES * j
        qblk = p[:, _M_DQ + lo:_M_DQ + lo + LANES] * cos_d + p[:, _M_DQR + lo:_M_DQR + lo + LANES] * sin_d
        qd_ref[:, lo:lo + LANES] = (qblk * (LOG2E * HEAD_DIM ** -0.5)).astype(BF16)
        kblk = p[:, _M_DK + lo:_M_DK + lo + LANES] * cos_d + p[:, _M_DKR + lo:_M_DKR + lo + LANES] * sin_d
        kd_ref[:, lo:lo + LANES] = kblk.astype(BF16)
    vd_ref[...] = p[:, _M_DV:_M_DV + MIX_COLS].astype(BF16)


def _proj_call(x2d, t_len, g1, wm, wt, gq, wuq, gkv, wukt, wuv, tab, tabt, tm):
    n = x2d.shape[0]
    nt = t_len // tm
    full = lambda a: pl.BlockSpec(a.shape, lambda i: (0,) * a.ndim)
    row = lambda w: pl.BlockSpec((tm, w), lambda i: (i, 0))
    col = lambda r: pl.BlockSpec((r, tm), lambda i: (0, i))
    tok = lambda w: jax.ShapeDtypeStruct((n, w), BF16)
    out_shape = (tok(N_HEADS * A_PAD), jax.ShapeDtypeStruct((N_HEADS * A_PAD, n), BF16), tok(MIX_COLS),
                 tok(MIX_COLS), tok(MIX_COLS), tok(MIX_COLS),
                 tok(MIX_COLS), jax.ShapeDtypeStruct((MIX_COLS, n), BF16), tok(MIX_COLS),
                 tok(MIX_COLS), tok(MIX_COLS), tok(MIX_COLS))
    out_specs = (row(N_HEADS * A_PAD), col(N_HEADS * A_PAD), row(MIX_COLS),
                 row(MIX_COLS), row(MIX_COLS), row(MIX_COLS),
                 row(MIX_COLS), col(MIX_COLS), row(MIX_COLS),
                 row(MIX_COLS), row(MIX_COLS), row(MIX_COLS))
    in_specs = [row(D_MODEL), full(g1), full(wm), full(wt), full(gq), full(wuq), full(gkv), full(wukt), full(wuv),
                pl.BlockSpec((tm, tab.shape[1]), lambda i: (i % nt, 0)),
                pl.BlockSpec((tabt.shape[0], tm), lambda i: (0, i % nt))]
    return pl.pallas_call(
        _proj_kernel, grid=(n // tm,), in_specs=in_specs, out_specs=out_specs, out_shape=out_shape,
        compiler_params=_cparams(("parallel",)), name="proj_in",
    )(x2d, g1, wm, wt, gq, wuq, gkv, wukt, wuv, tab, tabt)


def _row_total(l_lanes):
    return jnp.sum(l_lanes, axis=1, keepdims=True)


def _online_step(s, v, m_ref, l_ref, acc_ref, idx, first, keep=None):
    m_prev = jnp.where(first, -jnp.inf, m_ref[idx])
    m_new = jnp.maximum(m_prev, jnp.max(s, axis=1, keepdims=True))
    alpha = jnp.exp2(m_prev - m_new)
    p = jnp.exp2(s - m_new)
    part = p[:, 0:LANES]
    for c in range(1, s.shape[1] // LANES):
        part = part + p[:, LANES * c:LANES * (c + 1)]
    l_new = alpha * l_ref[idx] + part
    acc_new = alpha * acc_ref[idx] + jnp.dot(p.astype(BF16), v, preferred_element_type=F32)
    if keep is not None:
        l_new, acc_new = l_new * keep, acc_new * keep
        m_new = jnp.where(keep > 0.0, m_new, -jnp.inf)
    l_ref[idx] = l_new
    acc_ref[idx] = acc_new
    m_ref[idx] = m_new


def _zero_state_once(refs):
    @pl.when((pl.program_id(0) == 0) & (pl.program_id(1) == 0) & (pl.program_id(2) == 0))
    def _():
        for r in refs:
            r[...] = jnp.zeros(r.shape, r.dtype)


def _dense_maps(n_maps, defer, score, v_ref, vprev_ref, m_ref, l_ref, acc_ref, s_ref, finish):
    j = pl.program_id(2)
    first = j == 0
    is_last = j == pl.num_programs(2) - 1
    _zero_state_once([m_ref, l_ref, acc_ref] + ([s_ref] if defer else []))
    if defer:
        keep = jnp.where(first, 0.0, 1.0)
        _online_step(s_ref[...], vprev_ref[...], m_ref, l_ref, acc_ref, n_maps - 1, first, keep)
    v = v_ref[...]
    for mi in range(n_maps - 1 if defer else n_maps):
        _online_step(score(mi), v, m_ref, l_ref, acc_ref, mi, first)
    if defer:
        s_ref[...] = score(n_maps - 1)

    @pl.when(is_last)
    def _():
        if defer:
            _online_step(s_ref[...], v_ref[...], m_ref, l_ref, acc_ref, n_maps - 1, False)
        finish()


def _mla_kernel(defer, q_ref, kt_ref, v_ref, *refs):
    vprev_ref, s_ref = (refs[0], refs[-1]) if defer else (None, None)
    o_ref, m_ref, l_ref, acc_ref = refs[1:5] if defer else refs

    def score(hd):
        lo = A_PAD * hd
        return jnp.dot(q_ref[:, lo:lo + A_PAD], kt_ref[lo:lo + A_PAD, :], preferred_element_type=F32)

    def finish():
        head = _head_of_lane(o_ref.shape)
        out = jnp.zeros(o_ref.shape, F32)
        for hd in range(N_HEADS):
            out = out + jnp.where(head == hd, acc_ref[hd] * (1.0 / _row_total(l_ref[hd])), 0.0)
        o_ref[...] = out.astype(o_ref.dtype)

    _dense_maps(N_HEADS, defer, score, v_ref, vprev_ref, m_ref, l_ref, acc_ref, s_ref, finish)


def _dense_specs(bsz, t_len, tq, tk, q_cols, k_rows, defer):
    nq, nk = t_len // tq, t_len // tk
    specs = [pl.BlockSpec((tq, q_cols), lambda b, i, j: (b * nq + i, 0)),
             pl.BlockSpec((k_rows, tk), lambda b, i, j: (0, b * nk + j)),
             pl.BlockSpec((tk, MIX_COLS), lambda b, i, j: (b * nk + j, 0))]
    if defer:
        specs.append(pl.BlockSpec((tk, MIX_COLS), lambda b, i, j: (b * nk + jnp.maximum(j - 1, 0), 0)))
    out_spec = pl.BlockSpec((tq, MIX_COLS), lambda b, i, j: (b * nq + i, 0))
    return (bsz, nq, nk), specs, out_spec


def _dense_scratch(n_maps, tq, tk, defer):
    shapes = [pltpu.VMEM((n_maps, tq, 1), F32), pltpu.VMEM((n_maps, tq, LANES), F32),
              pltpu.VMEM((n_maps, tq, MIX_COLS), F32)]
    return shapes + ([pltpu.VMEM((tq, tk), F32)] if defer else [])


def _mla_call(q, kt, v, bsz, t_len, tq, tk):
    n = q.shape[0]
    defer = t_len // tk > 2
    grid, in_specs, out_spec = _dense_specs(bsz, t_len, tq, tk, N_HEADS * A_PAD, N_HEADS * A_PAD, defer)
    return pl.pallas_call(
        functools.partial(_mla_kernel, defer), grid=grid, in_specs=in_specs, out_specs=out_spec,
        out_shape=jax.ShapeDtypeStruct((n, MIX_COLS), BF16),
        scratch_shapes=_dense_scratch(N_HEADS, tq, tk, defer),
        compiler_params=_cparams(("arbitrary", "arbitrary", "arbitrary")), name="attn_mla",
    )(*((q, kt, v, v) if defer else (q, kt, v)))


def _diff_kernel(lam_init, defer, q_ref, kt_ref, v_ref, *refs):
    vprev_ref, s_ref = (refs[0], refs[-1]) if defer else (None, None)
    lam_ref, g_ref, o_ref, qm_ref, m_ref, l_ref, acc_ref = refs[1:8] if defer else refs
    n_maps = 2 * N_HEADS

    @pl.when(pl.program_id(2) == 0)
    def _():
        q = q_ref[...]
        group = lax.broadcasted_iota(jnp.int32, q.shape, 1) // C_DIM
        for mi in range(n_maps):
            qm_ref[mi] = jnp.where(group == mi, q, jnp.zeros_like(q))

    def score(mi):
        return jnp.dot(qm_ref[mi], kt_ref[...], preferred_element_type=F32)

    def finish():
        lam = lam_ref[...]
        head = _head_of_lane(o_ref.shape)
        o = jnp.zeros(o_ref.shape, F32)
        for hd in range(N_HEADS):
            oh = (acc_ref[2 * hd] * (1.0 / _row_total(l_ref[2 * hd]))
                  - lam * (acc_ref[2 * hd + 1] * (1.0 / _row_total(l_ref[2 * hd + 1]))))
            o = o + jnp.where(head == hd, oh, 0.0)
        o2 = o * o
        inv = jnp.zeros(o_ref.shape, F32)
        for hd in range(N_HEADS):
            ms = jnp.sum(jnp.where(head == hd, o2, 0.0), axis=1, keepdims=True) * (1.0 / HEAD_DIM)
            inv = inv + jnp.where(head == hd, lax.rsqrt(ms + RMS_EPS), 0.0)
        o_ref[...] = ((o * inv * g_ref[...]) * (1.0 - lam_init)).astype(o_ref.dtype)

    _dense_maps(n_maps, defer, score, v_ref, vprev_ref, m_ref, l_ref, acc_ref, s_ref, finish)


def _diff_call(q, kt, v, lam, g_tiled, lam_init, bsz, t_len, tq, tk):
    n = q.shape[0]
    n_maps = 2 * N_HEADS
    defer = t_len // tk > 2
    grid, in_specs, out_spec = _dense_specs(bsz, t_len, tq, tk, MIX_COLS, MIX_COLS, defer)
    in_specs += [pl.BlockSpec((1, 1), lambda b, i, j: (0, 0)), pl.BlockSpec((1, MIX_COLS), lambda b, i, j: (0, 0))]
    return pl.pallas_call(
        functools.partial(_diff_kernel, lam_init, defer), grid=grid, in_specs=in_specs, out_specs=out_spec,
        out_shape=jax.ShapeDtypeStruct((n, MIX_COLS), BF16),
        scratch_shapes=[pltpu.VMEM((n_maps, tq, MIX_COLS), BF16)] + _dense_scratch(n_maps, tq, tk, defer),
        compiler_params=_cparams(("arbitrary", "arbitrary", "arbitrary")), name="attn_diff",
    )(*((q, kt, v, v, lam, g_tiled) if defer else (q, kt, v, lam, g_tiled)))


def _window_heads(q, kw, vw, add, mult, o_ref):
    head = _head_of_lane(q.shape)
    out = jnp.zeros(q.shape, F32)
    for hd in range(N_HEADS):
        qh = jnp.where(head == hd, q, jnp.zeros_like(q))
        s = _nt_dot(qh, kw)
        if add is not None:
            s = s + add(hd)
        if mult is not None:
            s = jnp.where(mult > 0.0, s, NEG_BIG)
        m = jnp.max(s, axis=1, keepdims=True)
        p = jnp.exp2(s - m)
        if mult is not None:
            p = p * mult
        l = jnp.sum(p, axis=1, keepdims=True)
        o = jnp.dot(p.astype(BF16), vw, preferred_element_type=F32)
        out = out + jnp.where(head == hd, o * (1.0 / l), 0.0)
    o_ref[...] = out.astype(o_ref.dtype)


def _nbr_kernel(nb, q_ref, k_ref, v_ref, bias_ref, o_ref):
    i = pl.program_id(1)
    start = pl.multiple_of(jnp.clip(i - 1, 0, nb - B_WIN // QBLK) * QBLK, QBLK)
    kw = k_ref[pl.ds(start, B_WIN), :]
    vw = v_ref[pl.ds(start, B_WIN), :]
    _window_heads(q_ref[...], kw, vw, lambda hd: bias_ref[0, hd], None, o_ref)


def _nbr_block_type(i, nb):
    return jnp.where(i < 1, 0, jnp.where(i > nb - 2, 2, 1))


def _nbr_call(q, k, v, bias, bsz, t_len):
    n = q.shape[0]
    nb = t_len // QBLK
    return pl.pallas_call(
        functools.partial(_nbr_kernel, nb), grid=(bsz, nb),
        in_specs=[pl.BlockSpec((QBLK, MIX_COLS), lambda b, i: (b * nb + i, 0)),
                  pl.BlockSpec((t_len, MIX_COLS), lambda b, i: (b, 0)),
                  pl.BlockSpec((t_len, MIX_COLS), lambda b, i: (b, 0)),
                  pl.BlockSpec((1, N_HEADS, QBLK, B_WIN), lambda b, i: (_nbr_block_type(i, nb), 0, 0, 0))],
        out_specs=pl.BlockSpec((QBLK, MIX_COLS), lambda b, i: (b * nb + i, 0)),
        out_shape=jax.ShapeDtypeStruct((n, MIX_COLS), BF16),
        compiler_params=_cparams(("parallel", "arbitrary")), name="attn_nbr",
    )(q, k, v, bias)


def _nbr_bias_tables(rpb, t_len):
    rows = t_len // GRID_W
    nb = t_len // QBLK
    kr = min(NA_ROWS, rows)
    reps = (0, 1, nb - 1)
    n_dr, n_dc = 2 * NA_ROWS - 1, 2 * NA_COLS - 1
    q_rows, w_rows = QBLK // GRID_W, B_WIN // GRID_W
    qc = np.arange(GRID_W)[:, None]
    kc = np.arange(GRID_W)[None, :]
    cs = np.clip(qc - NA_COLS // 2, 0, GRID_W - NA_COLS)
    col_ok = (kc >= cs) & (kc < cs + NA_COLS)
    sel_c = ((kc - qc + NA_COLS - 1)[None] == np.arange(n_dc)[:, None, None]) & col_ok[None]
    sel_r = np.zeros((len(reps), q_rows, w_rows, n_dr), np.float32)
    row_ok = np.zeros((len(reps), q_rows, w_rows), bool)
    for ti, b in enumerate(reps):
        start_row = int(np.clip(b - 1, 0, nb - B_WIN // QBLK)) * q_rows
        for qr in range(q_rows):
            r = b * q_rows + qr
            rs = int(np.clip(r - kr // 2, 0, rows - kr))
            for wr in range(w_rows):
                key_r = start_row + wr
                if rs <= key_r < rs + kr:
                    row_ok[ti, qr, wr] = True
                    sel_r[ti, qr, wr, key_r - r + NA_ROWS - 1] = 1.0
    hp = lax.Precision.HIGHEST
    by_col = jnp.einsum("hrd,dqk->hrqk", rpb.astype(F32), jnp.asarray(sel_c, F32), precision=hp)
    dense = jnp.einsum("tawr,hrqk->thaqwk", jnp.asarray(sel_r), by_col, precision=hp) * LOG2E
    ok = row_ok[:, None, :, None, :, None] & col_ok[None, None, None, :, None, :]
    return jnp.where(ok, dense, NEG_BIG).reshape(len(reps), N_HEADS, QBLK, B_WIN)


def _dil_kernel(t_len, tq, q_ref, k_ref, v_ref, o_ref):
    i = pl.program_id(1)
    w = tq + 2 * D_REACH
    t0 = i * tq
    start = pl.multiple_of(jnp.clip(t0 - D_REACH, 0, t_len - w), tq)
    kw = k_ref[pl.ds(start, w), :]
    vw = v_ref[pl.ds(start, w), :]
    d = (start - t0) + lax.broadcasted_iota(jnp.int32, (tq, w), 1) - lax.broadcasted_iota(jnp.int32, (tq, w), 0)
    ad = jnp.abs(d)
    mult = ((ad <= 64).astype(F32)
            + (((d & 3) == 0) & (ad <= 256)).astype(F32)
            + (((d & 15) == 0) & (ad <= D_REACH)).astype(F32))
    _window_heads(q_ref[...], kw, vw, None, mult, o_ref)


def _dil_call(q, k, v, bsz, t_len, tq):
    n = q.shape[0]
    nq = t_len // tq
    return pl.pallas_call(
        functools.partial(_dil_kernel, t_len, tq), grid=(bsz, nq),
        in_specs=[pl.BlockSpec((tq, MIX_COLS), lambda b, i: (b * nq + i, 0)),
                  pl.BlockSpec((t_len, MIX_COLS), lambda b, i: (b, 0)),
                  pl.BlockSpec((t_len, MIX_COLS), lambda b, i: (b, 0))],
        out_specs=pl.BlockSpec((tq, MIX_COLS), lambda b, i: (b * nq + i, 0)),
        out_shape=jax.ShapeDtypeStruct((n, MIX_COLS), BF16),
        compiler_params=_cparams(("parallel", "arbitrary")), name="attn_dil",
    )(q, k, v)


def _out_kernel(oa_ref, ob_ref, oc_ref, od_ref, x_ref, wo_ref, g2_ref, wr_ref, x2_ref, hn_ref, aff_ref):
    acc = x_ref[...]
    for mi, o_ref in enumerate((oa_ref, ob_ref, oc_ref, od_ref)):
        acc = acc + jnp.dot(o_ref[...], wo_ref[MIX_COLS * mi:MIX_COLS * (mi + 1), :], preferred_element_type=F32)
    x2_ref[...] = acc
    hn = _rms(acc) * g2_ref[...]
    hn_hi = hn.astype(BF16)
    hn_ref[...] = hn_hi
    hn_lo = (hn - hn_hi.astype(F32)).astype(BF16)
    logits = (jnp.dot(hn_hi, wr_ref[0], preferred_element_type=F32)
              + jnp.dot(hn_lo, wr_ref[0], preferred_element_type=F32)
              + jnp.dot(hn_hi, wr_ref[1], preferred_element_type=F32))
    lane = lax.broadcasted_iota(jnp.int32, logits.shape, 1)
    logits = jnp.where(lane < N_EXPERTS, logits, -jnp.inf)
    m = jnp.max(logits, axis=1, keepdims=True)
    e = jnp.exp(logits - m)
    aff_ref[...] = e / jnp.sum(e, axis=1, keepdims=True)


def _out_call(oa, ob, oc, od, x2d, wo, g2, wr, tm):
    n = x2d.shape[0]
    full = lambda a: pl.BlockSpec(a.shape, lambda i: (0,) * a.ndim)
    row = lambda w: pl.BlockSpec((tm, w), lambda i: (i, 0))
    return pl.pallas_call(
        _out_kernel, grid=(n // tm,),
        in_specs=[row(MIX_COLS)] * 4 + [row(D_MODEL), full(wo), full(g2), full(wr)],
        out_specs=(row(D_MODEL), row(D_MODEL), row(LANES)),
        out_shape=(jax.ShapeDtypeStruct((n, D_MODEL), F32), jax.ShapeDtypeStruct((n, D_MODEL), BF16),
                   jax.ShapeDtypeStruct((n, LANES), F32)),
        compiler_params=_cparams(("parallel",)), name="proj_out",
    )(oa, ob, oc, od, x2d, wo, g2, wr)


FLAG_FIRST, FLAG_LAST, FLAG_SUB = 1, 2, 4
Y_BLK = 128
GATHER_FANIN = 4
COMBINE_FANIN = 8


def _count_le(sorted_vals, queries):
    return jnp.sum((sorted_vals[None, :] <= queries[:, None]).astype(jnp.int32), axis=1)


def _step_lists(cnt, fanin, n_steps):
    steps = jnp.maximum((cnt + fanin - 1) // fanin, 1)
    ends = jnp.cumsum(steps)
    w = jnp.arange(n_steps, dtype=jnp.int32)
    grp = jnp.minimum(_count_le(ends, w), cnt.shape[0] - 1)
    per_grp = jnp.stack([ends - steps, steps, cnt], axis=1)[grp]
    rank = w - per_grp[:, 0]
    valid = w < ends[-1]
    item0 = rank * fanin
    flags = (jnp.where(valid & (rank == 0), FLAG_FIRST, 0)
             | jnp.where(valid & (rank == per_grp[:, 1] - 1), FLAG_LAST, 0))
    for k in range(fanin):
        flags = flags | jnp.where(valid & (item0 + k < per_grp[:, 2]), FLAG_SUB << k, 0)
    return grp, item0, flags.astype(jnp.int32)


def _route_tables(aff, n, cap, ts, tc, tt):
    gate, idx = lax.top_k(aff[:, :N_EXPERTS].T, cap)
    idx, gate = lax.sort((idx, gate), dimension=1, num_keys=1)
    idx = idx.astype(jnp.int32)

    tiles_per_e = cap // ts
    n_tiles = N_EXPERTS * tiles_per_e
    n_chunks = n // tc
    c0 = (idx[:, ::ts] // tc).reshape(n_tiles)
    c1 = (idx[:, ts - 1::ts] // tc).reshape(n_tiles)
    n_gs = (N_EXPERTS * n_chunks + n_tiles) // GATHER_FANIN + n_tiles
    g_tile, g_item0, g_flags = _step_lists(c1 - c0 + 1, GATHER_FANIN, n_gs)
    g_first = c0[g_tile] + g_item0
    g_chunks = jnp.concatenate([jnp.minimum(g_first + k, n_chunks - 1) for k in range(GATHER_FANIN)])

    n_tt = n // tt
    blk_per_e = cap // Y_BLK
    bounds = jnp.arange(n_tt + 1, dtype=jnp.int32) * tt
    pos = jnp.sum((idx[:, None, :] < bounds[None, :, None]).astype(jnp.int32), axis=2)
    lo, hi = pos[:, :-1], pos[:, 1:]
    b0 = jnp.minimum(lo // Y_BLK, blk_per_e - 1)
    b1 = jnp.maximum((hi - 1) // Y_BLK, b0)
    pair_cnt = jnp.where(hi > lo, b1 - b0 + 1, 0).T.reshape(-1)
    pair_b0 = (b0 + (jnp.arange(N_EXPERTS, dtype=jnp.int32) * blk_per_e)[:, None]).T.reshape(-1)
    n_items = N_EXPERTS * blk_per_e + N_EXPERTS * n_tt
    item_ends = jnp.cumsum(pair_cnt)
    it = jnp.arange(n_items, dtype=jnp.int32)
    pair = jnp.minimum(_count_le(item_ends, it), pair_cnt.shape[0] - 1)
    item_blk = jnp.minimum(pair_b0[pair] + it - (item_ends - pair_cnt)[pair], N_EXPERTS * blk_per_e - 1)
    tile_cnt = jnp.sum(pair_cnt.reshape(n_tt, N_EXPERTS), axis=1)
    tile_item0 = jnp.cumsum(tile_cnt) - tile_cnt
    n_cs = n_items // COMBINE_FANIN + n_tt
    c_tile, c_item0, c_flags = _step_lists(tile_cnt, COMBINE_FANIN, n_cs)
    c_first = tile_item0[c_tile] + c_item0
    c_blks = jnp.concatenate([item_blk[jnp.minimum(c_first + k, n_items - 1)] for k in range(COMBINE_FANIN)])
    return idx, gate, (g_tile, g_chunks, g_flags), (c_tile, c_blks, c_flags)


def _ffn_kernel(tc, tile_ref, chunk_ref, flag_ref, tok_ref, gate_ref, *refs):
    hn_refs = refs[:GATHER_FANIN]
    wg_ref, wu_ref, wd_ref, y_ref, xacc = refs[GATHER_FANIN:]
    w = pl.program_id(0)
    n_steps = pl.num_programs(0)
    flags = flag_ref[w]

    @pl.when((flags & FLAG_FIRST) != 0)
    def _():
        xacc[...] = jnp.zeros(xacc.shape, F32)

    for k in range(GATHER_FANIN):
        @pl.when((flags & (FLAG_SUB << k)) != 0)
        def _(k=k):
            ts = tok_ref.shape[0]
            token = chunk_ref[k * n_steps + w] * tc + lax.broadcasted_iota(jnp.int32, (ts, tc), 1)
            onehot = jnp.where(tok_ref[...] == token, 1.0, 0.0).astype(BF16)
            xacc[...] += jnp.dot(onehot, hn_refs[k][...], preferred_element_type=F32)

    @pl.when((flags & FLAG_LAST) != 0)
    def _():
        xg = xacc[...].astype(BF16)
        a = jnp.dot(xg, wg_ref[0, 0], preferred_element_type=F32)
        b = jnp.dot(xg, wu_ref[0, 0], preferred_element_type=F32)
        hid = (a * jax.nn.sigmoid(a) * b).astype(BF16)
        y_ref[...] = (jnp.dot(hid, wd_ref[0, 0], preferred_element_type=F32) * gate_ref[...]).astype(BF16)


def _ffn_call(steps, idx, gate, hn, w_gate, w_up, w_down, layer, ts, tc):
    n_exp, cap = idx.shape
    tiles_per_e = cap // ts
    tile, chunks, flags = steps
    n_steps = tile.shape[0]
    tok = idx.reshape(n_exp * cap, 1)
    gate2 = gate.reshape(n_exp * cap, 1)
    slot = lambda wd: pl.BlockSpec((ts, wd), lambda w, tile, chunks, flags: (tile[w], 0))
    chunk = lambda k: pl.BlockSpec((tc, D_MODEL), lambda w, tile, chunks, flags: (chunks[k * n_steps + w], 0))
    wspec = pl.BlockSpec((1, 1, D_MODEL, D_MODEL), lambda w, tile, chunks, flags: (layer, tile[w] // tiles_per_e, 0, 0))
    grid_spec = pltpu.PrefetchScalarGridSpec(
        num_scalar_prefetch=3, grid=(n_steps,),
        in_specs=[slot(1), slot(1)] + [chunk(k) for k in range(GATHER_FANIN)] + [wspec, wspec, wspec],
        out_specs=slot(D_MODEL),
        scratch_shapes=[pltpu.VMEM((ts, D_MODEL), F32)])
    return pl.pallas_call(
        functools.partial(_ffn_kernel, tc), grid_spec=grid_spec,
        out_shape=jax.ShapeDtypeStruct((n_exp * cap, D_MODEL), BF16),
        compiler_params=_cparams(("arbitrary",)), name="expert_ffn",
    )(tile, chunks, flags, tok, gate2, *([hn] * GATHER_FANIN), w_gate, w_up, w_down)


def _combine_kernel(tt, final, tile_ref, blk_ref, flag_ref, *refs):
    tok_refs = refs[:COMBINE_FANIN]
    y_refs = refs[COMBINE_FANIN:2 * COMBINE_FANIN]
    x_ref, g_ref, o_ref = refs[2 * COMBINE_FANIN:]
    w = pl.program_id(0)
    flags = flag_ref[w]

    @pl.when((flags & FLAG_FIRST) != 0)
    def _():
        o_ref[...] = x_ref[...]

    @pl.when((flags & FLAG_SUB) != 0)
    def _():
        token = tile_ref[w] * tt + lax.broadcasted_iota(jnp.int32, (tt, Y_BLK), 0)
        hots = []
        for k in range(COMBINE_FANIN):
            live = (flags & (FLAG_SUB << k)) != 0
            tok = jnp.where(live, tok_refs[k][0], -1)
            hots.append(jnp.where(token == tok, 1.0, 0.0).astype(BF16))
        onehot = jnp.concatenate(hots, axis=1)
        ycat = jnp.concatenate([y_refs[k][...] for k in range(COMBINE_FANIN)], axis=0)
        o_ref[...] += jnp.dot(onehot, ycat, preferred_element_type=F32)

    if final:
        @pl.when((flags & FLAG_LAST) != 0)
        def _():
            o_ref[...] = _rms(o_ref[...]) * g_ref[...]


def _combine_call(steps, idx, y, x2, g_final, tt, final):
    n = x2.shape[0]
    tile, blks, flags = steps
    n_steps = tile.shape[0]
    tok = idx.reshape(-1, 1, Y_BLK)
    tspec = lambda k: pl.BlockSpec((1, 1, Y_BLK), lambda w, tile, blks, flags: (blks[k * n_steps + w], 0, 0))
    yspec = lambda k: pl.BlockSpec((Y_BLK, D_MODEL), lambda w, tile, blks, flags: (blks[k * n_steps + w], 0))
    xspec = pl.BlockSpec((tt, D_MODEL), lambda w, tile, blks, flags: (tile[w], 0))
    fan = range(COMBINE_FANIN)
    grid_spec = pltpu.PrefetchScalarGridSpec(
        num_scalar_prefetch=3, grid=(n_steps,),
        in_specs=[tspec(k) for k in fan] + [yspec(k) for k in fan]
        + [xspec, pl.BlockSpec((1, D_MODEL), lambda w, tile, blks, flags: (0, 0))],
        out_specs=xspec)
    return pl.pallas_call(
        functools.partial(_combine_kernel, tt, final), grid_spec=grid_spec,
        out_shape=jax.ShapeDtypeStruct((n, D_MODEL), F32),
        compiler_params=_cparams(("arbitrary",)), name="expert_combine",
    )(tile, blks, flags, *([tok] * COMBINE_FANIN), *([y] * COMBINE_FANIN), x2, g_final)


def _rot_cols(w, d):
    k, c = w.shape
    half = d // 2
    wg = w.reshape(k, c // d, 2, half)
    return jnp.concatenate([-wg[:, :, 1], wg[:, :, 0]], axis=2).reshape(k, c)


def _rope_tables(t_len):
    pos = jnp.arange(t_len, dtype=F32)

    def cs(d):
        half = d // 2
        inv = ROPE_THETA ** (-jnp.arange(half, dtype=F32) / half)
        ang = pos[:, None] * inv[None, :]
        return (jnp.concatenate([jnp.cos(ang)] * 2, axis=1), jnp.concatenate([jnp.sin(ang)] * 2, axis=1))

    c32, s32 = cs(A_ROPE)
    c64, s64 = cs(HEAD_DIM)
    ones = jnp.ones((t_len, A_NOPE), F32)
    zeros = jnp.zeros((t_len, A_NOPE), F32)
    pad = jnp.zeros((t_len, A_PAD - A_NOPE - A_ROPE), F32)
    tab = jnp.concatenate([ones, c32, pad, zeros, s32, pad,
                           jnp.tile(c32, (1, LANES // A_ROPE)), jnp.tile(s32, (1, LANES // A_ROPE)),
                           jnp.tile(c64, (1, LANES // HEAD_DIM)), jnp.tile(s64, (1, LANES // HEAD_DIM))], axis=1)
    tabt = jnp.concatenate([c32.T, s32.T], axis=0)
    return tab, tabt


def _layer_weights(l, w_in, a_w_uq, a_w_ukv, w_out, w_router):
    pts = np.cumsum(PROJ_SIZES)[:-1]
    (a_cq, a_ckv, a_kr, b_q, b_k, b_v, c_q, c_k, c_v, d_q, d_k, d_v) = jnp.split(w_in[l], pts, axis=1)
    wm = jnp.concatenate([a_cq, a_ckv, b_q, b_k, b_v, c_q, _rot_cols(c_q, C_DIM), c_v,
                          d_q, _rot_cols(d_q, HEAD_DIM), d_k, _rot_cols(d_k, HEAD_DIM), d_v], axis=1).astype(BF16)
    wt = jnp.concatenate([c_k, _rot_cols(c_k, C_DIM), a_kr, _rot_cols(a_kr, A_ROPE)], axis=1).T.astype(BF16)
    uq = a_w_uq[l].reshape(A_Q_LORA, N_HEADS, A_NOPE + A_ROPE)
    zpad = jnp.zeros((A_Q_LORA, N_HEADS, A_PAD - A_NOPE - A_ROPE), F32)
    uq_rope = uq[:, :, A_NOPE:]
    uq_rot = _rot_cols(uq_rope.reshape(A_Q_LORA, N_HEADS * A_ROPE), A_ROPE).reshape(A_Q_LORA, N_HEADS, A_ROPE)
    plain = jnp.concatenate([uq, zpad], axis=2).reshape(A_Q_LORA, N_HEADS * A_PAD)
    rot = jnp.concatenate([jnp.zeros_like(uq[:, :, :A_NOPE]), uq_rot, zpad], axis=2).reshape(A_Q_LORA, N_HEADS * A_PAD)
    wuq = jnp.concatenate([plain, rot], axis=1).astype(BF16)
    ukv = a_w_ukv[l].reshape(A_KV_LORA, N_HEADS, 2 * HEAD_DIM)
    wukt = ukv[:, :, :A_NOPE].reshape(A_KV_LORA, MIX_COLS).T.astype(BF16)
    wuv = ukv[:, :, A_NOPE:].reshape(A_KV_LORA, MIX_COLS).astype(BF16)
    wo = w_out[l].astype(BF16)
    wr = jnp.concatenate([w_router[l], jnp.zeros((D_MODEL, LANES - N_EXPERTS), F32)], axis=1)
    wr_hi = wr.astype(BF16)
    wr = jnp.stack([wr_hi, (wr - wr_hi.astype(F32)).astype(BF16)])
    return wm, wt, wuq, wukt, wuv, wo, wr


def _trunk(x, norm1_g, w_in, a_qnorm_g, a_w_uq, a_kvnorm_g, a_w_ukv, b_rpb, c_lambda, c_subln_g, w_out,
           norm2_g, w_router, w_gate, w_up, w_down, final_g):
    bsz, t_len, _ = x.shape
    n = bsz * t_len
    tm = 512
    tq = 512
    tk_mla, tk_diff = min(4096, t_len), 2048
    tq_dil = 256
    ts, tc, tt = 256, 1024, 512
    depth = w_in.shape[0]
    cap = CAP_FACTOR * n // N_EXPERTS
    assert t_len % tk_mla == 0 and t_len % tk_diff == 0 and t_len >= tq_dil + 2 * D_REACH and t_len // QBLK >= 3
    assert cap % ts == 0 and n % tc == 0 and n % tt == 0
    tab, tabt = _rope_tables(t_len)
    x2d = x.reshape(n, D_MODEL)
    g_final = final_g.reshape(1, D_MODEL)
    for l in range(depth):
        lam_init = 0.8 - 0.6 * math.exp(-0.3 * l)
        wm, wt, wuq, wukt, wuv, wo, wr = _layer_weights(l, w_in, a_w_uq, a_w_ukv, w_out, w_router)
        row = lambda g: g[l].reshape(1, -1)
        (qa, kat, va, qb, kb, vb, qc, kct, vc, qd, kd, vd) = _proj_call(
            x2d, t_len, row(norm1_g), wm, wt, row(a_qnorm_g), wuq, row(a_kvnorm_g), wukt, wuv, tab, tabt, tm)
        oa = _mla_call(qa, kat, va, bsz, t_len, tq, tk_mla)
        ob = _nbr_call(qb, kb, vb, _nbr_bias_tables(b_rpb[l], t_len), bsz, t_len)
        lp = c_lambda[l].astype(F32)
        lam = (jnp.exp(jnp.sum(lp[0] * lp[1])) - jnp.exp(jnp.sum(lp[2] * lp[3])) + lam_init).reshape(1, 1)
        g_sub = jnp.tile(c_subln_g[l], N_HEADS).reshape(1, MIX_COLS)
        oc = _diff_call(qc, kct, vc, lam, g_sub, lam_init, bsz, t_len, tq, tk_diff)
        od = _dil_call(qd, kd, vd, bsz, t_len, tq_dil)
        x2, hn, aff = _out_call(oa, ob, oc, od, x2d, wo, row(norm2_g), wr, tm)
        idx, gate, g_items, c_items = _route_tables(aff, n, cap, ts, tc, tt)
        y = _ffn_call(g_items, idx, gate, hn, w_gate, w_up, w_down, l, ts, tc)
        x2d = _combine_call(c_items, idx, y, x2, g_final, tt, l == depth - 1)
    return x2d.reshape(bsz, t_len, D_MODEL)


def kernel(x_prompt, x_sample, norm1_g, w_in, a_qnorm_g, a_w_uq, a_kvnorm_g, a_w_ukv, b_rpb, c_lambda, c_subln_g,
           w_out, norm2_g, w_router, w_gate, w_up, w_down, final_g):
    params = (norm1_g, w_in, a_qnorm_g, a_w_uq, a_kvnorm_g, a_w_ukv, b_rpb, c_lambda, c_subln_g, w_out,
              norm2_g, w_router, w_gate.astype(BF16), w_up.astype(BF16), w_down.astype(BF16), final_g)
    return (_trunk(x_prompt, *params), _trunk(x_sample, *params))
```

```python
import functools
import math

import numpy as np
import jax
import jax.numpy as jnp
from jax import lax
from jax.experimental import pallas as pl
from jax.experimental.pallas import tpu as pltpu

F32 = jnp.float32
BF16 = jnp.bfloat16

D_MODEL = 1024
N_HEADS = 4
HEAD_DIM = 64
MIX_COLS = N_HEADS * HEAD_DIM
A_NOPE = 64
A_ROPE = 32
A_PAD = 128
A_Q_LORA = 256
A_KV_LORA = 128
C_DIM = 32
GRID_W = 64
NA_ROWS = 8
NA_COLS = 16
QBLK = 256
B_WIN = 3 * QBLK
D_REACH = 1024
N_EXPERTS = 16
CAP_FACTOR = 2
ROPE_THETA = 10000.0
RMS_EPS = 1e-6
NEG_BIG = -1e30
LOG2E = math.log2(math.e)
LANES = 128
V7X_VMEM_LIMIT = 56 * 1024 * 1024

PROJ_SIZES = (256, 128, 32, 256, 256, 256, 256, 256, 256, 256, 256, 256)

_M_ACQ, _M_ACKV, _M_BQ, _M_BK, _M_BV = 0, 256, 384, 640, 896
_M_CQ, _M_CQR, _M_CV = 1152, 1408, 1664
_M_DQ, _M_DQR, _M_DK, _M_DKR, _M_DV = 1920, 2176, 2432, 2688, 2944
_M_COLS = 3200
_T_CK, _T_CKR, _T_AKR, _T_AKRR, _T_ROWS = 0, 256, 512, 544, 576


def _cparams(sem):
    return pltpu.CompilerParams(dimension_semantics=sem, vmem_limit_bytes=V7X_VMEM_LIMIT)


def _rms(x):
    return x * lax.rsqrt(jnp.mean(x * x, axis=-1, keepdims=True) + RMS_EPS)


def _nt_dot(a, b):
    return lax.dot_general(a, b, (((1,), (1,)), ((), ())), preferred_element_type=F32)


def _head_of_lane(shape):
    return lax.broadcasted_iota(jnp.int32, shape, len(shape) - 1) // HEAD_DIM


def _proj_kernel(x_ref, g1_ref, wm_ref, wt_ref, gq_ref, wuq_ref, gkv_ref, wukt_ref, wuv_ref,
                 tab_ref, tabt_ref,
                 qa_ref, kat_ref, va_ref, qb_ref, kb_ref, vb_ref, qc_ref, kct_ref, vc_ref,
                 qd_ref, kd_ref, vd_ref):
    x = x_ref[...]
    h = (_rms(x) * g1_ref[...]).astype(BF16)
    p = jnp.dot(h, wm_ref[...], preferred_element_type=F32)
    pt = _nt_dot(wt_ref[...], h)
    tab = tab_ref[...]
    cos_a, sin_a, cos_c, sin_c, cos_d, sin_d = (tab[:, LANES * i:LANES * (i + 1)] for i in range(6))
    tabt = tabt_ref[...]
    cos_t, sin_t = tabt[0:A_ROPE], tabt[A_ROPE:2 * A_ROPE]

    scale_a = LOG2E * (A_NOPE + A_ROPE) ** -0.5
    latq = (_rms(p[:, _M_ACQ:_M_ACQ + A_Q_LORA]) * gq_ref[...]).astype(BF16)
    qa2 = jnp.dot(latq, wuq_ref[...], preferred_element_type=F32)
    for hd in range(N_HEADS):
        lo = A_PAD * hd
        blk = qa2[:, lo:lo + A_PAD] * cos_a + qa2[:, N_HEADS * A_PAD + lo:N_HEADS * A_PAD + lo + A_PAD] * sin_a
        qa_ref[:, lo:lo + A_PAD] = (blk * scale_a).astype(BF16)
    latkv = (_rms(p[:, _M_ACKV:_M_ACKV + A_KV_LORA]) * gkv_ref[...]).astype(BF16)
    va_ref[...] = jnp.dot(latkv, wuv_ref[...], preferred_element_type=F32).astype(BF16)
    knt = _nt_dot(wukt_ref[...], latkv)
    krt = (pt[_T_AKR:_T_AKR + A_ROPE] * cos_t + pt[_T_AKRR:_T_AKRR + A_ROPE] * sin_t).astype(BF16)
    tm = x.shape[0]
    for hd in range(N_HEADS):
        lo = A_PAD * hd
        kat_ref[lo:lo + A_NOPE, :] = knt[A_NOPE * hd:A_NOPE * (hd + 1)].astype(BF16)
        kat_ref[lo + A_NOPE:lo + A_NOPE + A_ROPE, :] = krt
        kat_ref[lo + A_NOPE + A_ROPE:lo + A_PAD, :] = jnp.zeros((A_PAD - A_NOPE - A_ROPE, tm), BF16)

    qb_ref[...] = (p[:, _M_BQ:_M_BQ + MIX_COLS] * (LOG2E * HEAD_DIM ** -0.5)).astype(BF16)
    kb_ref[...] = p[:, _M_BK:_M_BK + MIX_COLS].astype(BF16)
    vb_ref[...] = p[:, _M_BV:_M_BV + MIX_COLS].astype(BF16)

    scale_c = LOG2E * C_DIM ** -0.5
    for j in range(MIX_COLS // LANES):
        lo = LANES * j
        blk = p[:, _M_CQ + lo:_M_CQ + lo + LANES] * cos_c + p[:, _M_CQR + lo:_M_CQR + lo + LANES] * sin_c
        qc_ref[:, lo:lo + LANES] = (blk * scale_c).astype(BF16)
    reps = MIX_COLS // A_ROPE
    cos_ct = jnp.concatenate([cos_t] * reps, axis=0)
    sin_ct = jnp.concatenate([sin_t] * reps, axis=0)
    kct_ref[...] = (pt[_T_CK:_T_CK + MIX_COLS] * cos_ct + pt[_T_CKR:_T_CKR + MIX_COLS] * sin_ct).astype(BF16)
    vc_ref[...] = p[:, _M_CV:_M_CV + MIX_COLS].astype(BF16)

    for j in range(MIX_COLS // LANES):
        lo = LANES * j
        qblk = p[:, _M_DQ + lo:_M_DQ + lo + LANES] * cos_d + p[:, _M_DQR + lo:_M_DQR + lo + LANES] * sin_d
        qd_ref[:, lo:lo + LANES] = (qblk * (LOG2E * HEAD_DIM ** -0.5)).astype(BF16)
        kblk = p[:, _M_DK + lo:_M_DK + lo + LANES] * cos_d + p[:, _M_DKR + lo:_M_DKR + lo + LANES] * sin_d
        kd_ref[:, lo:lo + LANES] = kblk.astype(BF16)
    vd_ref[...] = p[:, _M_DV:_M_DV + MIX_COLS].astype(BF16)


def _proj_call(x2d, t_len, g1, wm, wt, gq, wuq, gkv, wukt, wuv, tab, tabt, tm):
    n = x2d.shape[0]
    nt = t_len // tm
    full = lambda a: pl.BlockSpec(a.shape, lambda i: (0,) * a.ndim)
    row = lambda w: pl.BlockSpec((tm, w), lambda i: (i, 0))
    col = lambda r: pl.BlockSpec((r, tm), lambda i: (0, i))
    tok = lambda w: jax.ShapeDtypeStruct((n, w), BF16)
    out_shape = (tok(N_HEADS * A_PAD), jax.ShapeDtypeStruct((N_HEADS * A_PAD, n), BF16), tok(MIX_COLS),
                 tok(MIX_COLS), tok(MIX_COLS), tok(MIX_COLS),
                 tok(MIX_COLS), jax.ShapeDtypeStruct((MIX_COLS, n), BF16), tok(MIX_COLS),
                 tok(MIX_COLS), tok(MIX_COLS), tok(MIX_COLS))
    out_specs = (row(N_HEADS * A_PAD), col(N_HEADS * A_PAD), row(MIX_COLS),
                 row(MIX_COLS), row(MIX_COLS), row(MIX_COLS),
                 row(MIX_COLS), col(MIX_COLS), row(MIX_COLS),
                 row(MIX_COLS), row(MIX_COLS), row(MIX_COLS))
    in_specs = [row(D_MODEL), full(g1), full(wm), full(wt), full(gq), full(wuq), full(gkv), full(wukt), full(wuv),
                pl.BlockSpec((tm, tab.shape[1]), lambda i: (i % nt, 0)),
                pl.BlockSpec((tabt.shape[0], tm), lambda i: (0, i % nt))]
    return pl.pallas_call(
        _proj_kernel, grid=(n // tm,), in_specs=in_specs, out_specs=out_specs, out_shape=out_shape,
        compiler_params=_cparams(("parallel",)), name="proj_in",
    )(x2d, g1, wm, wt, gq, wuq, gkv, wukt, wuv, tab, tabt)


def _row_total(l_lanes):
    return jnp.sum(l_lanes, axis=1, keepdims=True)


def _online_step(s, v, m_ref, l_ref, acc_ref, idx, first, keep=None):
    m_prev = jnp.where(first, -jnp.inf, m_ref[idx])
    m_new = jnp.maximum(m_prev, jnp.max(s, axis=1, keepdims=True))
    alpha = jnp.exp2(m_prev - m_new)
    p = jnp.exp2(s - m_new)
    part = p[:, 0:LANES]
    for c in range(1, s.shape[1] // LANES):
        part = part + p[:, LANES * c:LANES * (c + 1)]
    l_new = alpha * l_ref[idx] + part
    acc_new = alpha * acc_ref[idx] + jnp.dot(p.astype(BF16), v, preferred_element_type=F32)
    if keep is not None:
        l_new, acc_new = l_new * keep, acc_new * keep
        m_new = jnp.where(keep > 0.0, m_new, -jnp.inf)
    l_ref[idx] = l_new
    acc_ref[idx] = acc_new
    m_ref[idx] = m_new


def _zero_state_once(refs):
    @pl.when((pl.program_id(0) == 0) & (pl.program_id(1) == 0) & (pl.program_id(2) == 0))
    def _():
        for r in refs:
            r[...] = jnp.zeros(r.shape, r.dtype)


def _dense_maps(n_maps, defer, score, v_ref, vprev_ref, m_ref, l_ref, acc_ref, s_ref, finish):
    j = pl.program_id(2)
    first = j == 0
    is_last = j == pl.num_programs(2) - 1
    _zero_state_once([m_ref, l_ref, acc_ref] + ([s_ref] if defer else []))
    if defer:
        keep = jnp.where(first, 0.0, 1.0)
        _online_step(s_ref[...], vprev_ref[...], m_ref, l_ref, acc_ref, n_maps - 1, first, keep)
    v = v_ref[...]
    for mi in range(n_maps - 1 if defer else n_maps):
        _online_step(score(mi), v, m_ref, l_ref, acc_ref, mi, first)
    if defer:
        s_ref[...] = score(n_maps - 1)

    @pl.when(is_last)
    def _():
        if defer:
            _online_step(s_ref[...], v_ref[...], m_ref, l_ref, acc_ref, n_maps - 1, False)
        finish()


def _mla_kernel(defer, q_ref, kt_ref, v_ref, *refs):
    vprev_ref, s_ref = (refs[0], refs[-1]) if defer else (None, None)
    o_ref, m_ref, l_ref, acc_ref = refs[1:5] if defer else refs

    def score(hd):
        lo = A_PAD * hd
        return jnp.dot(q_ref[:, lo:lo + A_PAD], kt_ref[lo:lo + A_PAD, :], preferred_element_type=F32)

    def finish():
        head = _head_of_lane(o_ref.shape)
        out = jnp.zeros(o_ref.shape, F32)
        for hd in range(N_HEADS):
            out = out + jnp.where(head == hd, acc_ref[hd] * (1.0 / _row_total(l_ref[hd])), 0.0)
        o_ref[...] = out.astype(o_ref.dtype)

    _dense_maps(N_HEADS, defer, score, v_ref, vprev_ref, m_ref, l_ref, acc_ref, s_ref, finish)


def _dense_specs(bsz, t_len, tq, tk, q_cols, k_rows, defer):
    nq, nk = t_len // tq, t_len // tk
    specs = [pl.BlockSpec((tq, q_cols), lambda b, i, j: (b * nq + i, 0)),
             pl.BlockSpec((k_rows, tk), lambda b, i, j: (0, b * nk + j)),
             pl.BlockSpec((tk, MIX_COLS), lambda b, i, j: (b * nk + j, 0))]
    if defer:
        specs.append(pl.BlockSpec((tk, MIX_COLS), lambda b, i, j: (b * nk + jnp.maximum(j - 1, 0), 0)))
    out_spec = pl.BlockSpec((tq, MIX_COLS), lambda b, i, j: (b * nq + i, 0))
    return (bsz, nq, nk), specs, out_spec


def _dense_scratch(n_maps, tq, tk, defer):
    shapes = [pltpu.VMEM((n_maps, tq, 1), F32), pltpu.VMEM((n_maps, tq, LANES), F32),
              pltpu.VMEM((n_maps, tq, MIX_COLS), F32)]
    return shapes + ([pltpu.VMEM((tq, tk), F32)] if defer else [])


def _mla_call(q, kt, v, bsz, t_len, tq, tk):
    n = q.shape[0]
    defer = t_len // tk > 2
    grid, in_specs, out_spec = _dense_specs(bsz, t_len, tq, tk, N_HEADS * A_PAD, N_HEADS * A_PAD, defer)
    return pl.pallas_call(
        functools.partial(_mla_kernel, defer), grid=grid, in_specs=in_specs, out_specs=out_spec,
        out_shape=jax.ShapeDtypeStruct((n, MIX_COLS), BF16),
        scratch_shapes=_dense_scratch(N_HEADS, tq, tk, defer),
        compiler_params=_cparams(("arbitrary", "arbitrary", "arbitrary")), name="attn_mla",
    )(*((q, kt, v, v) if defer else (q, kt, v)))


def _diff_kernel(lam_init, defer, q_ref, kt_ref, v_ref, *refs):
    vprev_ref, s_ref = (refs[0], refs[-1]) if defer else (None, None)
    lam_ref, g_ref, o_ref, qm_ref, m_ref, l_ref, acc_ref = refs[1:8] if defer else refs
    n_maps = 2 * N_HEADS

    @pl.when(pl.program_id(2) == 0)
    def _():
        q = q_ref[...]
        group = lax.broadcasted_iota(jnp.int32, q.shape, 1) // C_DIM
        for mi in range(n_maps):
            qm_ref[mi] = jnp.where(group == mi, q, jnp.zeros_like(q))

    def score(mi):
        return jnp.dot(qm_ref[mi], kt_ref[...], preferred_element_type=F32)

    def finish():
        lam = lam_ref[...]
        head = _head_of_lane(o_ref.shape)
        o = jnp.zeros(o_ref.shape, F32)
        for hd in range(N_HEADS):
            oh = (acc_ref[2 * hd] * (1.0 / _row_total(l_ref[2 * hd]))
                  - lam * (acc_ref[2 * hd + 1] * (1.0 / _row_total(l_ref[2 * hd + 1]))))
            o = o + jnp.where(head == hd, oh, 0.0)
        o2 = o * o
        inv = jnp.zeros(o_ref.shape, F32)
        for hd in range(N_HEADS):
            ms = jnp.sum(jnp.where(head == hd, o2, 0.0), axis=1, keepdims=True) * (1.0 / HEAD_DIM)
            inv = inv + jnp.where(head == hd, lax.rsqrt(ms + RMS_EPS), 0.0)
        o_ref[...] = ((o * inv * g_ref[...]) * (1.0 - lam_init)).astype(o_ref.dtype)

    _dense_maps(n_maps, defer, score, v_ref, vprev_ref, m_ref, l_ref, acc_ref, s_ref, finish)


def _diff_call(q, kt, v, lam, g_tiled, lam_init, bsz, t_len, tq, tk):
    n = q.shape[0]
    n_maps = 2 * N_HEADS
    defer = t_len // tk > 2
    grid, in_specs, out_spec = _dense_specs(bsz, t_len, tq, tk, MIX_COLS, MIX_COLS, defer)
    in_specs += [pl.BlockSpec((1, 1), lambda b, i, j: (0, 0)), pl.BlockSpec((1, MIX_COLS), lambda b, i, j: (0, 0))]
    return pl.pallas_call(
        functools.partial(_diff_kernel, lam_init, defer), grid=grid, in_specs=in_specs, out_specs=out_spec,
        out_shape=jax.ShapeDtypeStruct((n, MIX_COLS), BF16),
        scratch_shapes=[pltpu.VMEM((n_maps, tq, MIX_COLS), BF16)] + _dense_scratch(n_maps, tq, tk, defer),
        compiler_params=_cparams(("arbitrary", "arbitrary", "arbitrary")), name="attn_diff",
    )(*((q, kt, v, v, lam, g_tiled) if defer else (q, kt, v, lam, g_tiled)))


def _window_heads(q, kw, vw, add, mult, o_ref):
    head = _head_of_lane(q.shape)
    out = jnp.zeros(q.shape, F32)
    for hd in range(N_HEADS):
        qh = jnp.where(head == hd, q, jnp.zeros_like(q))
        s = _nt_dot(qh, kw)
        if add is not None:
            s = s + add(hd)
        if mult is not None:
            s = jnp.where(mult > 0.0, s, NEG_BIG)
        m = jnp.max(s, axis=1, keepdims=True)
        p = jnp.exp2(s - m)
        if mult is not None:
            p = p * mult
        l = jnp.sum(p, axis=1, keepdims=True)
        o = jnp.dot(p.astype(BF16), vw, preferred_element_type=F32)
        out = out + jnp.where(head == hd, o * (1.0 / l), 0.0)
    o_ref[...] = out.astype(o_ref.dtype)


def _nbr_kernel(nb, q_ref, k_ref, v_ref, bias_ref, o_ref):
    i = pl.program_id(1)
    start = pl.multiple_of(jnp.clip(i - 1, 0, nb - B_WIN // QBLK) * QBLK, QBLK)
    kw = k_ref[pl.ds(start, B_WIN), :]
    vw = v_ref[pl.ds(start, B_WIN), :]
    _window_heads(q_ref[...], kw, vw, lambda hd: bias_ref[0, hd], None, o_ref)


def _nbr_block_type(i, nb):
    return jnp.where(i < 1, 0, jnp.where(i > nb - 2, 2, 1))


def _nbr_call(q, k, v, bias, bsz, t_len):
    n = q.shape[0]
    nb = t_len // QBLK
    return pl.pallas_call(
        functools.partial(_nbr_kernel, nb), grid=(bsz, nb),
        in_specs=[pl.BlockSpec((QBLK, MIX_COLS), lambda b, i: (b * nb + i, 0)),
                  pl.BlockSpec((t_len, MIX_COLS), lambda b, i: (b, 0)),
                  pl.BlockSpec((t_len, MIX_COLS), lambda b, i: (b, 0)),
                  pl.BlockSpec((1, N_HEADS, QBLK, B_WIN), lambda b, i: (_nbr_block_type(i, nb), 0, 0, 0))],
        out_specs=pl.BlockSpec((QBLK, MIX_COLS), lambda b, i: (b * nb + i, 0)),
        out_shape=jax.ShapeDtypeStruct((n, MIX_COLS), BF16),
        compiler_params=_cparams(("parallel", "arbitrary")), name="attn_nbr",
    )(q, k, v, bias)


def _nbr_bias_tables(rpb, t_len):
    rows = t_len // GRID_W
    nb = t_len // QBLK
    kr = min(NA_ROWS, rows)
    reps = (0, 1, nb - 1)
    n_dr, n_dc = 2 * NA_ROWS - 1, 2 * NA_COLS - 1
    q_rows, w_rows = QBLK // GRID_W, B_WIN // GRID_W
    qc = np.arange(GRID_W)[:, None]
    kc = np.arange(GRID_W)[None, :]
    cs = np.clip(qc - NA_COLS // 2, 0, GRID_W - NA_COLS)
    col_ok = (kc >= cs) & (kc < cs + NA_COLS)
    sel_c = ((kc - qc + NA_COLS - 1)[None] == np.arange(n_dc)[:, None, None]) & col_ok[None]
    sel_r = np.zeros((len(reps), q_rows, w_rows, n_dr), np.float32)
    row_ok = np.zeros((len(reps), q_rows, w_rows), bool)
    for ti, b in enumerate(reps):
        start_row = int(np.clip(b - 1, 0, nb - B_WIN // QBLK)) * q_rows
        for qr in range(q_rows):
            r = b * q_rows + qr
            rs = int(np.clip(r - kr // 2, 0, rows - kr))
            for wr in range(w_rows):
                key_r = start_row + wr
                if rs <= key_r < rs + kr:
                    row_ok[ti, qr, wr] = True
                    sel_r[ti, qr, wr, key_r - r + NA_ROWS - 1] = 1.0
    hp = lax.Precision.HIGHEST
    by_col = jnp.einsum("hrd,dqk->hrqk", rpb.astype(F32), jnp.asarray(sel_c, F32), precision=hp)
    dense = jnp.einsum("tawr,hrqk->thaqwk", jnp.asarray(sel_r), by_col, precision=hp) * LOG2E
    ok = row_ok[:, None, :, None, :, None] & col_ok[None, None, None, :, None, :]
    return jnp.where(ok, dense, NEG_BIG).reshape(len(reps), N_HEADS, QBLK, B_WIN)


def _dil_kernel(t_len, tq, q_ref, k_ref, v_ref, mask_ref, o_ref):
    i = pl.program_id(1)
    w = tq + 2 * D_REACH
    start = pl.multiple_of(jnp.clip(i * tq - D_REACH, 0, t_len - w), tq)
    kw = k_ref[pl.ds(start, w), :]
    vw = v_ref[pl.ds(start, w), :]
    _window_heads(q_ref[...], kw, vw, lambda hd: mask_ref[0], None, o_ref)


def _dil_masks(tq):
    w = tq + 2 * D_REACH
    n_place = 2 * D_REACH // tq + 1
    place = jnp.arange(n_place, dtype=jnp.int32)[:, None, None] * tq
    d = (lax.broadcasted_iota(jnp.int32, (n_place, tq, w), 2) - place
         - lax.broadcasted_iota(jnp.int32, (n_place, tq, w), 1))
    ad = jnp.abs(d)
    count = ((ad <= 64).astype(F32)
             + (((d & 3) == 0) & (ad <= 256)).astype(F32)
             + (((d & 15) == 0) & (ad <= D_REACH)).astype(F32))
    return jnp.where(count > 0.0, jnp.log2(jnp.maximum(count, 1.0)), NEG_BIG)


def _dil_call(q, k, v, masks, bsz, t_len, tq):
    n = q.shape[0]
    nq = t_len // tq
    w = tq + 2 * D_REACH
    n_place = masks.shape[0]
    half = D_REACH // tq

    def placement(i):
        return jnp.where(i < half, i, jnp.where(i > nq - 1 - half, i - nq + n_place, half))

    whole = lambda: pl.BlockSpec((t_len, MIX_COLS), lambda b, i: (b, 0), pipeline_mode=pl.Buffered(1))
    return pl.pallas_call(
        functools.partial(_dil_kernel, t_len, tq), grid=(bsz, nq),
        in_specs=[pl.BlockSpec((tq, MIX_COLS), lambda b, i: (b * nq + i, 0)), whole(), whole(),
                  pl.BlockSpec((1, tq, w), lambda b, i: (placement(i), 0, 0))],
        out_specs=pl.BlockSpec((tq, MIX_COLS), lambda b, i: (b * nq + i, 0)),
        out_shape=jax.ShapeDtypeStruct((n, MIX_COLS), BF16),
        compiler_params=_cparams(("arbitrary", "arbitrary")), name="attn_dil",
    )(q, k, v, masks)


def _out_kernel(oa_ref, ob_ref, oc_ref, od_ref, x_ref, wo_ref, g2_ref, wr_ref, x2_ref, hn_ref, aff_ref):
    acc = x_ref[...]
    for mi, o_ref in enumerate((oa_ref, ob_ref, oc_ref, od_ref)):
        acc = acc + jnp.dot(o_ref[...], wo_ref[MIX_COLS * mi:MIX_COLS * (mi + 1), :], preferred_element_type=F32)
    x2_ref[...] = acc
    hn = _rms(acc) * g2_ref[...]
    hn_hi = hn.astype(BF16)
    hn_ref[...] = hn_hi
    hn_lo = (hn - hn_hi.astype(F32)).astype(BF16)
    logits = (jnp.dot(hn_hi, wr_ref[0], preferred_element_type=F32)
              + jnp.dot(hn_lo, wr_ref[0], preferred_element_type=F32)
              + jnp.dot(hn_hi, wr_ref[1], preferred_element_type=F32))
    lane = lax.broadcasted_iota(jnp.int32, logits.shape, 1)
    logits = jnp.where(lane < N_EXPERTS, logits, -jnp.inf)
    m = jnp.max(logits, axis=1, keepdims=True)
    e = jnp.exp(logits - m)
    aff_ref[...] = e / jnp.sum(e, axis=1, keepdims=True)


def _out_call(oa, ob, oc, od, x2d, wo, g2, wr, tm):
    n = x2d.shape[0]
    full = lambda a: pl.BlockSpec(a.shape, lambda i: (0,) * a.ndim)
    row = lambda w: pl.BlockSpec((tm, w), lambda i: (i, 0))
    return pl.pallas_call(
        _out_kernel, grid=(n // tm,),
        in_specs=[row(MIX_COLS)] * 4 + [row(D_MODEL), full(wo), full(g2), full(wr)],
        out_specs=(row(D_MODEL), row(D_MODEL), row(LANES)),
        out_shape=(jax.ShapeDtypeStruct((n, D_MODEL), F32), jax.ShapeDtypeStruct((n, D_MODEL), BF16),
                   jax.ShapeDtypeStruct((n, LANES), F32)),
        compiler_params=_cparams(("parallel",)), name="proj_out",
    )(oa, ob, oc, od, x2d, wo, g2, wr)


FLAG_FIRST, FLAG_LAST, FLAG_SUB = 1, 2, 4
Y_BLK = 128
GATHER_FANIN = 4
COMBINE_FANIN = 8


def _count_le(sorted_vals, queries):
    return jnp.sum((sorted_vals[None, :] <= queries[:, None]).astype(jnp.int32), axis=1)


def _step_lists(cnt, fanin, n_steps):
    steps = jnp.maximum((cnt + fanin - 1) // fanin, 1)
    ends = jnp.cumsum(steps)
    w = jnp.arange(n_steps, dtype=jnp.int32)
    grp = jnp.minimum(_count_le(ends, w), cnt.shape[0] - 1)
    per_grp = jnp.stack([ends - steps, steps, cnt], axis=1)[grp]
    rank = w - per_grp[:, 0]
    valid = w < ends[-1]
    item0 = rank * fanin
    flags = (jnp.where(valid & (rank == 0), FLAG_FIRST, 0)
             | jnp.where(valid & (rank == per_grp[:, 1] - 1), FLAG_LAST, 0))
    return grp, item0, flags, jnp.where(valid, per_grp[:, 2] - item0, 0)


def _with_live_flags(flags, remaining, slot_ranks):
    for k, r in enumerate(slot_ranks):
        flags = flags | jnp.where(r < remaining, FLAG_SUB << k, 0)
    return flags.astype(jnp.int32)


def _route_tables(aff, n, cap, ts, tc, tt):
    gate, idx = lax.top_k(aff[:, :N_EXPERTS].T, cap)
    idx, gate = lax.sort((idx, gate), dimension=1, num_keys=1)
    idx = idx.astype(jnp.int32)

    tiles_per_e = cap // ts
    n_tiles = N_EXPERTS * tiles_per_e
    n_chunks = n // tc
    c0 = (idx[:, ::ts] // tc).reshape(n_tiles)
    c1 = (idx[:, ts - 1::ts] // tc).reshape(n_tiles)
    n_gs = (N_EXPERTS * n_chunks + n_tiles) // GATHER_FANIN + n_tiles
    g_tile, g_item0, g_flags, g_left = _step_lists(c1 - c0 + 1, GATHER_FANIN, n_gs)
    g_first = c0[g_tile] + g_item0
    g_ranks = [(k - g_first) % GATHER_FANIN for k in range(GATHER_FANIN)]
    g_flags = _with_live_flags(g_flags, g_left, g_ranks)
    g_chunks = jnp.concatenate([jnp.minimum(g_first + r, n_chunks - 1) for r in g_ranks])

    n_tt = n // tt
    blk_per_e = cap // Y_BLK
    bounds = jnp.arange(n_tt + 1, dtype=jnp.int32) * tt
    pos = jnp.sum((idx[:, None, :] < bounds[None, :, None]).astype(jnp.int32), axis=2)
    lo, hi = pos[:, :-1], pos[:, 1:]
    b0 = jnp.minimum(lo // Y_BLK, blk_per_e - 1)
    b1 = jnp.maximum((hi - 1) // Y_BLK, b0)
    pair_cnt = jnp.where(hi > lo, b1 - b0 + 1, 0).T.reshape(-1)
    pair_b0 = (b0 + (jnp.arange(N_EXPERTS, dtype=jnp.int32) * blk_per_e)[:, None]).T.reshape(-1)
    n_items = N_EXPERTS * blk_per_e + N_EXPERTS * n_tt
    item_ends = jnp.cumsum(pair_cnt)
    it = jnp.arange(n_items, dtype=jnp.int32)
    pair = jnp.minimum(_count_le(item_ends, it), pair_cnt.shape[0] - 1)
    item_blk = jnp.minimum(pair_b0[pair] + it - (item_ends - pair_cnt)[pair], N_EXPERTS * blk_per_e - 1)
    tile_cnt = jnp.sum(pair_cnt.reshape(n_tt, N_EXPERTS), axis=1)
    tile_item0 = jnp.cumsum(tile_cnt) - tile_cnt
    n_cs = n_items // COMBINE_FANIN + n_tt
    c_tile, c_item0, c_flags, c_left = _step_lists(tile_cnt, COMBINE_FANIN, n_cs)
    c_flags = _with_live_flags(c_flags, c_left, list(range(COMBINE_FANIN)))
    c_first = tile_item0[c_tile] + c_item0
    c_blks = jnp.concatenate([item_blk[jnp.minimum(c_first + k, n_items - 1)] for k in range(COMBINE_FANIN)])
    return idx, gate, (g_tile, g_chunks, g_flags), (c_tile, c_blks, c_flags)


def _ffn_kernel(tc, tile_ref, chunk_ref, flag_ref, tok_ref, gate_ref, *refs):
    hn_refs = refs[:GATHER_FANIN]
    wg_ref, wu_ref, wd_ref, y_ref, xacc = refs[GATHER_FANIN:]
    w = pl.program_id(0)
    n_steps = pl.num_programs(0)
    flags = flag_ref[w]

    @pl.when((flags & FLAG_FIRST) != 0)
    def _():
        xacc[...] = jnp.zeros(xacc.shape, F32)

    for k in range(GATHER_FANIN):
        @pl.when((flags & (FLAG_SUB << k)) != 0)
        def _(k=k):
            ts = tok_ref.shape[0]
            token = chunk_ref[k * n_steps + w] * tc + lax.broadcasted_iota(jnp.int32, (ts, tc), 1)
            onehot = jnp.where(tok_ref[...] == token, 1.0, 0.0).astype(BF16)
            xacc[...] += jnp.dot(onehot, hn_refs[k][...], preferred_element_type=F32)

    @pl.when((flags & FLAG_LAST) != 0)
    def _():
        xg = xacc[...].astype(BF16)
        a = jnp.dot(xg, wg_ref[0, 0], preferred_element_type=F32)
        b = jnp.dot(xg, wu_ref[0, 0], preferred_element_type=F32)
        hid = (a * jax.nn.sigmoid(a) * b).astype(BF16)
        y_ref[...] = (jnp.dot(hid, wd_ref[0, 0], preferred_element_type=F32) * gate_ref[...]).astype(BF16)


def _ffn_call(steps, idx, gate, hn, w_gate, w_up, w_down, layer, ts, tc):
    n_exp, cap = idx.shape
    tiles_per_e = cap // ts
    tile, chunks, flags = steps
    n_steps = tile.shape[0]
    tok = idx.reshape(n_exp * cap, 1)
    gate2 = gate.reshape(n_exp * cap, 1)
    slot = lambda wd: pl.BlockSpec((ts, wd), lambda w, tile, chunks, flags: (tile[w], 0))
    chunk = lambda k: pl.BlockSpec((tc, D_MODEL), lambda w, tile, chunks, flags: (chunks[k * n_steps + w], 0))
    wspec = pl.BlockSpec((1, 1, D_MODEL, D_MODEL), lambda w, tile, chunks, flags: (layer, tile[w] // tiles_per_e, 0, 0))
    grid_spec = pltpu.PrefetchScalarGridSpec(
        num_scalar_prefetch=3, grid=(n_steps,),
        in_specs=[slot(1), slot(1)] + [chunk(k) for k in range(GATHER_FANIN)] + [wspec, wspec, wspec],
        out_specs=slot(D_MODEL),
        scratch_shapes=[pltpu.VMEM((ts, D_MODEL), F32)])
    return pl.pallas_call(
        functools.partial(_ffn_kernel, tc), grid_spec=grid_spec,
        out_shape=jax.ShapeDtypeStruct((n_exp * cap, D_MODEL), BF16),
        compiler_params=_cparams(("arbitrary",)), name="expert_ffn",
    )(tile, chunks, flags, tok, gate2, *([hn] * GATHER_FANIN), w_gate, w_up, w_down)


def _combine_kernel(tt, final, tile_ref, blk_ref, flag_ref, *refs):
    tok_refs = refs[:COMBINE_FANIN]
    y_refs = refs[COMBINE_FANIN:2 * COMBINE_FANIN]
    x_ref, g_ref, o_ref = refs[2 * COMBINE_FANIN:]
    w = pl.program_id(0)
    flags = flag_ref[w]

    @pl.when((flags & FLAG_FIRST) != 0)
    def _():
        o_ref[...] = x_ref[...]

    @pl.when((flags & FLAG_SUB) != 0)
    def _():
        token = tile_ref[w] * tt + lax.broadcasted_iota(jnp.int32, (tt, Y_BLK), 0)
        hots = []
        for k in range(COMBINE_FANIN):
            live = (flags & (FLAG_SUB << k)) != 0
            tok = jnp.where(live, tok_refs[k][0], -1)
            hots.append(jnp.where(token == tok, 1.0, 0.0).astype(BF16))
        onehot = jnp.concatenate(hots, axis=1)
        ycat = jnp.concatenate([y_refs[k][...] for k in range(COMBINE_FANIN)], axis=0)
        o_ref[...] += jnp.dot(onehot, ycat, preferred_element_type=F32)

    if final:
        @pl.when((flags & FLAG_LAST) != 0)
        def _():
            o_ref[...] = _rms(o_ref[...]) * g_ref[...]


def _combine_call(steps, idx, y, x2, g_final, tt, final):
    n = x2.shape[0]
    tile, blks, flags = steps
    n_steps = tile.shape[0]
    tok = idx.reshape(-1, 1, Y_BLK)
    tspec = lambda k: pl.BlockSpec((1, 1, Y_BLK), lambda w, tile, blks, flags: (blks[k * n_steps + w], 0, 0))
    yspec = lambda k: pl.BlockSpec((Y_BLK, D_MODEL), lambda w, tile, blks, flags: (blks[k * n_steps + w], 0))
    xspec = pl.BlockSpec((tt, D_MODEL), lambda w, tile, blks, flags: (tile[w], 0))
    fan = range(COMBINE_FANIN)
    grid_spec = pltpu.PrefetchScalarGridSpec(
        num_scalar_prefetch=3, grid=(n_steps,),
        in_specs=[tspec(k) for k in fan] + [yspec(k) for k in fan]
        + [xspec, pl.BlockSpec((1, D_MODEL), lambda w, tile, blks, flags: (0, 0))],
        out_specs=xspec)
    return pl.pallas_call(
        functools.partial(_combine_kernel, tt, final), grid_spec=grid_spec,
        out_shape=jax.ShapeDtypeStruct((n, D_MODEL), F32),
        compiler_params=_cparams(("arbitrary",)), name="expert_combine",
    )(tile, blks, flags, *([tok] * COMBINE_FANIN), *([y] * COMBINE_FANIN), x2, g_final)


def _rot_cols(w, d):
    k, c = w.shape
    half = d // 2
    wg = w.reshape(k, c // d, 2, half)
    return jnp.concatenate([-wg[:, :, 1], wg[:, :, 0]], axis=2).reshape(k, c)


def _rope_tables(t_len):
    pos = jnp.arange(t_len, dtype=F32)

    def cs(d):
        half = d // 2
        inv = ROPE_THETA ** (-jnp.arange(half, dtype=F32) / half)
        ang = pos[:, None] * inv[None, :]
        return (jnp.concatenate([jnp.cos(ang)] * 2, axis=1), jnp.concatenate([jnp.sin(ang)] * 2, axis=1))

    c32, s32 = cs(A_ROPE)
    c64, s64 = cs(HEAD_DIM)
    ones = jnp.ones((t_len, A_NOPE), F32)
    zeros = jnp.zeros((t_len, A_NOPE), F32)
    pad = jnp.zeros((t_len, A_PAD - A_NOPE - A_ROPE), F32)
    tab = jnp.concatenate([ones, c32, pad, zeros, s32, pad,
                           jnp.tile(c32, (1, LANES // A_ROPE)), jnp.tile(s32, (1, LANES // A_ROPE)),
                           jnp.tile(c64, (1, LANES // HEAD_DIM)), jnp.tile(s64, (1, LANES // HEAD_DIM))], axis=1)
    tabt = jnp.concatenate([c32.T, s32.T], axis=0)
    return tab, tabt


def _layer_weights(l, w_in, a_w_uq, a_w_ukv, w_out, w_router):
    pts = np.cumsum(PROJ_SIZES)[:-1]
    (a_cq, a_ckv, a_kr, b_q, b_k, b_v, c_q, c_k, c_v, d_q, d_k, d_v) = jnp.split(w_in[l], pts, axis=1)
    wm = jnp.concatenate([a_cq, a_ckv, b_q, b_k, b_v, c_q, _rot_cols(c_q, C_DIM), c_v,
                          d_q, _rot_cols(d_q, HEAD_DIM), d_k, _rot_cols(d_k, HEAD_DIM), d_v], axis=1).astype(BF16)
    wt = jnp.concatenate([c_k, _rot_cols(c_k, C_DIM), a_kr, _rot_cols(a_kr, A_ROPE)], axis=1).T.astype(BF16)
    uq = a_w_uq[l].reshape(A_Q_LORA, N_HEADS, A_NOPE + A_ROPE)
    zpad = jnp.zeros((A_Q_LORA, N_HEADS, A_PAD - A_NOPE - A_ROPE), F32)
    uq_rope = uq[:, :, A_NOPE:]
    uq_rot = _rot_cols(uq_rope.reshape(A_Q_LORA, N_HEADS * A_ROPE), A_ROPE).reshape(A_Q_LORA, N_HEADS, A_ROPE)
    plain = jnp.concatenate([uq, zpad], axis=2).reshape(A_Q_LORA, N_HEADS * A_PAD)
    rot = jnp.concatenate([jnp.zeros_like(uq[:, :, :A_NOPE]), uq_rot, zpad], axis=2).reshape(A_Q_LORA, N_HEADS * A_PAD)
    wuq = jnp.concatenate([plain, rot], axis=1).astype(BF16)
    ukv = a_w_ukv[l].reshape(A_KV_LORA, N_HEADS, 2 * HEAD_DIM)
    wukt = ukv[:, :, :A_NOPE].reshape(A_KV_LORA, MIX_COLS).T.astype(BF16)
    wuv = ukv[:, :, A_NOPE:].reshape(A_KV_LORA, MIX_COLS).astype(BF16)
    wo = w_out[l].astype(BF16)
    wr = jnp.concatenate([w_router[l], jnp.zeros((D_MODEL, LANES - N_EXPERTS), F32)], axis=1)
    wr_hi = wr.astype(BF16)
    wr = jnp.stack([wr_hi, (wr - wr_hi.astype(F32)).astype(BF16)])
    return wm, wt, wuq, wukt, wuv, wo, wr


def _trunk(x, norm1_g, w_in, a_qnorm_g, a_w_uq, a_kvnorm_g, a_w_ukv, b_rpb, c_lambda, c_subln_g, w_out,
           norm2_g, w_router, w_gate, w_up, w_down, final_g):
    bsz, t_len, _ = x.shape
    n = bsz * t_len
    tm = 512
    tq = 512
    tk_mla, tk_diff = min(4096, t_len), 2048
    tq_dil = 256
    ts, tc, tt = 256, 1024, 512
    depth = w_in.shape[0]
    cap = CAP_FACTOR * n // N_EXPERTS
    assert t_len % tk_mla == 0 and t_len % tk_diff == 0 and t_len >= tq_dil + 2 * D_REACH and t_len // QBLK >= 3
    assert cap % ts == 0 and n % tc == 0 and n % tt == 0
    tab, tabt = _rope_tables(t_len)
    dil_masks = _dil_masks(tq_dil)
    x2d = x.reshape(n, D_MODEL)
    g_final = final_g.reshape(1, D_MODEL)
    for l in range(depth):
        lam_init = 0.8 - 0.6 * math.exp(-0.3 * l)
        wm, wt, wuq, wukt, wuv, wo, wr = _layer_weights(l, w_in, a_w_uq, a_w_ukv, w_out, w_router)
        row = lambda g: g[l].reshape(1, -1)
        (qa, kat, va, qb, kb, vb, qc, kct, vc, qd, kd, vd) = _proj_call(
            x2d, t_len, row(norm1_g), wm, wt, row(a_qnorm_g), wuq, row(a_kvnorm_g), wukt, wuv, tab, tabt, tm)
        oa = _mla_call(qa, kat, va, bsz, t_len, tq, tk_mla)
        ob = _nbr_call(qb, kb, vb, _nbr_bias_tables(b_rpb[l], t_len), bsz, t_len)
        lp = c_lambda[l].astype(F32)
        lam = (jnp.exp(jnp.sum(lp[0] * lp[1])) - jnp.exp(jnp.sum(lp[2] * lp[3])) + lam_init).reshape(1, 1)
        g_sub = jnp.tile(c_subln_g[l], N_HEADS).reshape(1, MIX_COLS)
        oc = _diff_call(qc, kct, vc, lam, g_sub, lam_init, bsz, t_len, tq, tk_diff)
        od = _dil_call(qd, kd, vd, dil_masks, bsz, t_len, tq_dil)
        x2, hn, aff = _out_call(oa, ob, oc, od, x2d, wo, row(norm2_g), wr, tm)
        idx, gate, g_items, c_items = _route_tables(aff, n, cap, ts, tc, tt)
        y = _ffn_call(g_items, idx, gate, hn, w_gate, w_up, w_down, l, ts, tc)
        x2d = _combine_call(c_items, idx, y, x2, g_final, tt, l == depth - 1)
    return x2d.reshape(bsz, t_len, D_MODEL)


def kernel(x_prompt, x_sample, norm1_g, w_in, a_qnorm_g, a_w_uq, a_kvnorm_g, a_w_ukv, b_rpb, c_lambda, c_subln_g,
           w_out, norm2_g, w_router, w_gate, w_up, w_down, final_g):
    params = (norm1_g, w_in, a_qnorm_g, a_w_uq, a_kvnorm_g, a_w_ukv, b_rpb, c_lambda, c_subln_g, w_out,
              norm2_g, w_router, w_gate.astype(BF16), w_up.astype(BF16), w_down.astype(BF16), final_g)
    return (_trunk(x_prompt, *params), _trunk(x_sample, *params))
```

```python
import functools
import math

import numpy as np
import jax
import jax.numpy as jnp
from jax import lax
from jax.experimental import pallas as pl
from jax.experimental.pallas import tpu as pltpu

F32 = jnp.float32
BF16 = jnp.bfloat16

D_MODEL = 1024
N_HEADS = 4
HEAD_DIM = 64
MIX_COLS = N_HEADS * HEAD_DIM
A_NOPE = 64
A_ROPE = 32
A_PAD = 128
A_Q_LORA = 256
A_KV_LORA = 128
C_DIM = 32
GRID_W = 64
NA_ROWS = 8
NA_COLS = 16
QBLK = 256
B_WIN = 3 * QBLK
D_REACH = 1024
N_EXPERTS = 16
CAP_FACTOR = 2
ROPE_THETA = 10000.0
RMS_EPS = 1e-6
NEG_BIG = -1e30
LOG2E = math.log2(math.e)
LANES = 128
V7X_VMEM_LIMIT = 56 * 1024 * 1024

PROJ_SIZES = (256, 128, 32, 256, 256, 256, 256, 256, 256, 256, 256, 256)

_M_ACQ, _M_ACKV, _M_BQ, _M_BK, _M_BV = 0, 256, 384, 640, 896
_M_CQ, _M_CQR, _M_CV = 1152, 1408, 1664
_M_DQ, _M_DQR, _M_DK, _M_DKR, _M_DV = 1920, 2176, 2432, 2688, 2944
_M_COLS = 3200
_T_CK, _T_CKR, _T_AKR, _T_AKRR, _T_ROWS = 0, 256, 512, 544, 576


def _cparams(sem):
    return pltpu.CompilerParams(dimension_semantics=sem, vmem_limit_bytes=V7X_VMEM_LIMIT)


def _rms(x):
    return x * lax.rsqrt(jnp.mean(x * x, axis=-1, keepdims=True) + RMS_EPS)


def _nt_dot(a, b):
    return lax.dot_general(a, b, (((1,), (1,)), ((), ())), preferred_element_type=F32)


def _head_of_lane(shape):
    return lax.broadcasted_iota(jnp.int32, shape, len(shape) - 1) // HEAD_DIM


def _proj_kernel(x_ref, g1_ref, wm_ref, wt_ref, gq_ref, wuq_ref, gkv_ref, wukt_ref, wuv_ref,
                 tab_ref, tabt_ref,
                 qa_ref, kat_ref, va_ref, qb_ref, kb_ref, vb_ref, qc_ref, kct_ref, vc_ref,
                 qd_ref, kd_ref, vd_ref):
    x = x_ref[...]
    h = (_rms(x) * g1_ref[...]).astype(BF16)
    p = jnp.dot(h, wm_ref[...], preferred_element_type=F32)
    pt = _nt_dot(wt_ref[...], h)
    tab = tab_ref[...]
    cos_a, sin_a, cos_c, sin_c, cos_d, sin_d = (tab[:, LANES * i:LANES * (i + 1)] for i in range(6))
    tabt = tabt_ref[...]
    cos_t, sin_t = tabt[0:A_ROPE], tabt[A_ROPE:2 * A_ROPE]

    scale_a = LOG2E * (A_NOPE + A_ROPE) ** -0.5
    latq = (_rms(p[:, _M_ACQ:_M_ACQ + A_Q_LORA]) * gq_ref[...]).astype(BF16)
    qa2 = jnp.dot(latq, wuq_ref[...], preferred_element_type=F32)
    for hd in range(N_HEADS):
        lo = A_PAD * hd
        blk = qa2[:, lo:lo + A_PAD] * cos_a + qa2[:, N_HEADS * A_PAD + lo:N_HEADS * A_PAD + lo + A_PAD] * sin_a
        qa_ref[:, lo:lo + A_PAD] = (blk * scale_a).astype(BF16)
    latkv = (_rms(p[:, _M_ACKV:_M_ACKV + A_KV_LORA]) * gkv_ref[...]).astype(BF16)
    va_ref[...] = jnp.dot(latkv, wuv_ref[...], preferred_element_type=F32).astype(BF16)
    knt = _nt_dot(wukt_ref[...], latkv)
    krt = (pt[_T_AKR:_T_AKR + A_ROPE] * cos_t + pt[_T_AKRR:_T_AKRR + A_ROPE] * sin_t).astype(BF16)
    tm = x.shape[0]
    for hd in range(N_HEADS):
        lo = A_PAD * hd
        kat_ref[lo:lo + A_NOPE, :] = knt[A_NOPE * hd:A_NOPE * (hd + 1)].astype(BF16)
        kat_ref[lo + A_NOPE:lo + A_NOPE + A_ROPE, :] = krt
        kat_ref[lo + A_NOPE + A_ROPE:lo + A_PAD, :] = jnp.zeros((A_PAD - A_NOPE - A_ROPE, tm), BF16)

    qb_ref[...] = (p[:, _M_BQ:_M_BQ + MIX_COLS] * (LOG2E * HEAD_DIM ** -0.5)).astype(BF16)
    kb_ref[...] = p[:, _M_BK:_M_BK + MIX_COLS].astype(BF16)
    vb_ref[...] = p[:, _M_BV:_M_BV + MIX_COLS].astype(BF16)

    scale_c = LOG2E * C_DIM ** -0.5
    for j in range(MIX_COLS // LANES):
        lo = LANES * j
        blk = p[:, _M_CQ + lo:_M_CQ + lo + LANES] * cos_c + p[:, _M_CQR + lo:_M_CQR + lo + LANES] * sin_c
        qc_ref[:, lo:lo + LANES] = (blk * scale_c).astype(BF16)
    reps = MIX_COLS // A_ROPE
    cos_ct = jnp.concatenate([cos_t] * reps, axis=0)
    sin_ct = jnp.concatenate([sin_t] * reps, axis=0)
    kct_ref[...] = (pt[_T_CK:_T_CK + MIX_COLS] * cos_ct + pt[_T_CKR:_T_CKR + MIX_COLS] * sin_ct).astype(BF16)
    vc_ref[...] = p[:, _M_CV:_M_CV + MIX_COLS].astype(BF16)

    for j in range(MIX_COLS // LANES):
        lo = LANES * j
        qblk = p[:, _M_DQ + lo:_M_DQ + lo + LANES] * cos_d + p[:, _M_DQR + lo:_M_DQR + lo + LANES] * sin_d
        qd_ref[:, lo:lo + LANES] = (qblk * (LOG2E * HEAD_DIM ** -0.5)).astype(BF16)
        kblk = p[:, _M_DK + lo:_M_DK + lo + LANES] * cos_d + p[:, _M_DKR + lo:_M_DKR + lo + LANES] * sin_d
        kd_ref[:, lo:lo + LANES] = kblk.astype(BF16)
    vd_ref[...] = p[:, _M_DV:_M_DV + MIX_COLS].astype(BF16)


def _proj_call(x2d, t_len, g1, wm, wt, gq, wuq, gkv, wukt, wuv, tab, tabt, tm):
    n = x2d.shape[0]
    nt = t_len // tm
    full = lambda a: pl.BlockSpec(a.shape, lambda i: (0,) * a.ndim)
    row = lambda w: pl.BlockSpec((tm, w), lambda i: (i, 0))
    col = lambda r: pl.BlockSpec((r, tm), lambda i: (0, i))
    tok = lambda w: jax.ShapeDtypeStruct((n, w), BF16)
    out_shape = (tok(N_HEADS * A_PAD), jax.ShapeDtypeStruct((N_HEADS * A_PAD, n), BF16), tok(MIX_COLS),
                 tok(MIX_COLS), tok(MIX_COLS), tok(MIX_COLS),
                 tok(MIX_COLS), jax.ShapeDtypeStruct((MIX_COLS, n), BF16), tok(MIX_COLS),
                 tok(MIX_COLS), tok(MIX_COLS), tok(MIX_COLS))
    out_specs = (row(N_HEADS * A_PAD), col(N_HEADS * A_PAD), row(MIX_COLS),
                 row(MIX_COLS), row(MIX_COLS), row(MIX_COLS),
                 row(MIX_COLS), col(MIX_COLS), row(MIX_COLS),
                 row(MIX_COLS), row(MIX_COLS), row(MIX_COLS))
    in_specs = [row(D_MODEL), full(g1), full(wm), full(wt), full(gq), full(wuq), full(gkv), full(wukt), full(wuv),
                pl.BlockSpec((tm, tab.shape[1]), lambda i: (i % nt, 0)),
                pl.BlockSpec((tabt.shape[0], tm), lambda i: (0, i % nt))]
    return pl.pallas_call(
        _proj_kernel, grid=(n // tm,), in_specs=in_specs, out_specs=out_specs, out_shape=out_shape,
        compiler_params=_cparams(("parallel",)), name="proj_in",
    )(x2d, g1, wm, wt, gq, wuq, gkv, wukt, wuv, tab, tabt)


def _row_total(l_lanes):
    return jnp.sum(l_lanes, axis=1, keepdims=True)


def _online_step(s, v, m_ref, l_ref, acc_ref, idx, first, keep=None):
    m_prev = jnp.where(first, -jnp.inf, m_ref[idx])
    m_new = jnp.maximum(m_prev, jnp.max(s, axis=1, keepdims=True))
    alpha = jnp.exp2(m_prev - m_new)
    p = jnp.exp2(s - m_new)
    part = p[:, 0:LANES]
    for c in range(1, s.shape[1] // LANES):
        part = part + p[:, LANES * c:LANES * (c + 1)]
    l_new = alpha * l_ref[idx] + part
    acc_new = alpha * acc_ref[idx] + jnp.dot(p.astype(BF16), v, preferred_element_type=F32)
    if keep is not None:
        l_new, acc_new = l_new * keep, acc_new * keep
        m_new = jnp.where(keep > 0.0, m_new, -jnp.inf)
    l_ref[idx] = l_new
    acc_ref[idx] = acc_new
    m_ref[idx] = m_new


def _zero_state_once(refs):
    @pl.when((pl.program_id(0) == 0) & (pl.program_id(1) == 0) & (pl.program_id(2) == 0))
    def _():
        for r in refs:
            r[...] = jnp.zeros(r.shape, r.dtype)


def _dense_maps(n_maps, defer, score, v_block, m_ref, l_ref, acc_ref, s_ref, finish):
    j = pl.program_id(2)
    first = j == 0
    is_last = j == pl.num_programs(2) - 1
    _zero_state_once([m_ref, l_ref, acc_ref] + ([s_ref] if defer else []))
    if defer:
        keep = jnp.where(first, 0.0, 1.0)
        _online_step(s_ref[...], v_block(jnp.maximum(j - 1, 0)), m_ref, l_ref, acc_ref, n_maps - 1, first, keep)
    v = v_block(j)
    for mi in range(n_maps - 1 if defer else n_maps):
        _online_step(score(mi), v, m_ref, l_ref, acc_ref, mi, first)
    if defer:
        s_ref[...] = score(n_maps - 1)

    @pl.when(is_last)
    def _():
        if defer:
            _online_step(s_ref[...], v_block(j), m_ref, l_ref, acc_ref, n_maps - 1, False)
        finish()


def _value_block(v_ref, tk):
    return lambda j: v_ref[pl.ds(pl.multiple_of(j * tk, tk), tk), :]


def _mla_kernel(defer, q_ref, kt_ref, v_ref, o_ref, m_ref, l_ref, acc_ref, *s_ref):
    s_ref = s_ref[0] if defer else None

    def score(hd):
        lo = A_PAD * hd
        return jnp.dot(q_ref[:, lo:lo + A_PAD], kt_ref[lo:lo + A_PAD, :], preferred_element_type=F32)

    def finish():
        head = _head_of_lane(o_ref.shape)
        out = jnp.zeros(o_ref.shape, F32)
        for hd in range(N_HEADS):
            out = out + jnp.where(head == hd, acc_ref[hd] * (1.0 / _row_total(l_ref[hd])), 0.0)
        o_ref[...] = out.astype(o_ref.dtype)

    _dense_maps(N_HEADS, defer, score, _value_block(v_ref, kt_ref.shape[1]), m_ref, l_ref, acc_ref, s_ref, finish)


def _dense_specs(bsz, t_len, tq, tk, q_cols, k_rows):
    nq, nk = t_len // tq, t_len // tk
    specs = [pl.BlockSpec((tq, q_cols), lambda b, i, j: (b * nq + i, 0)),
             pl.BlockSpec((k_rows, tk), lambda b, i, j: (0, b * nk + j)),
             pl.BlockSpec((t_len, MIX_COLS), lambda b, i, j: (b, 0), pipeline_mode=pl.Buffered(1))]
    out_spec = pl.BlockSpec((tq, MIX_COLS), lambda b, i, j: (b * nq + i, 0))
    return (bsz, nq, nk), specs, out_spec


def _dense_scratch(n_maps, tq, tk, defer):
    shapes = [pltpu.VMEM((n_maps, tq, 1), F32), pltpu.VMEM((n_maps, tq, LANES), F32),
              pltpu.VMEM((n_maps, tq, MIX_COLS), F32)]
    return shapes + ([pltpu.VMEM((tq, tk), F32)] if defer else [])


def _mla_call(q, kt, v, bsz, t_len, tq, tk):
    n = q.shape[0]
    defer = t_len // tk > 2
    grid, in_specs, out_spec = _dense_specs(bsz, t_len, tq, tk, N_HEADS * A_PAD, N_HEADS * A_PAD)
    return pl.pallas_call(
        functools.partial(_mla_kernel, defer), grid=grid, in_specs=in_specs, out_specs=out_spec,
        out_shape=jax.ShapeDtypeStruct((n, MIX_COLS), BF16),
        scratch_shapes=_dense_scratch(N_HEADS, tq, tk, defer),
        compiler_params=_cparams(("arbitrary", "arbitrary", "arbitrary")), name="attn_mla",
    )(q, kt, v)


def _diff_kernel(lam_init, defer, q_ref, kt_ref, v_ref, lam_ref, g_ref, o_ref, qm_ref, m_ref, l_ref, acc_ref, *s_ref):
    s_ref = s_ref[0] if defer else None
    n_maps = 2 * N_HEADS

    @pl.when(pl.program_id(2) == 0)
    def _():
        q = q_ref[...]
        group = lax.broadcasted_iota(jnp.int32, q.shape, 1) // C_DIM
        for mi in range(n_maps):
            qm_ref[mi] = jnp.where(group == mi, q, jnp.zeros_like(q))

    def score(mi):
        return jnp.dot(qm_ref[mi], kt_ref[...], preferred_element_type=F32)

    def finish():
        lam = lam_ref[...]
        head = _head_of_lane(o_ref.shape)
        o = jnp.zeros(o_ref.shape, F32)
        for hd in range(N_HEADS):
            oh = (acc_ref[2 * hd] * (1.0 / _row_total(l_ref[2 * hd]))
                  - lam * (acc_ref[2 * hd + 1] * (1.0 / _row_total(l_ref[2 * hd + 1]))))
            o = o + jnp.where(head == hd, oh, 0.0)
        o2 = o * o
        inv = jnp.zeros(o_ref.shape, F32)
        for hd in range(N_HEADS):
            ms = jnp.sum(jnp.where(head == hd, o2, 0.0), axis=1, keepdims=True) * (1.0 / HEAD_DIM)
            inv = inv + jnp.where(head == hd, lax.rsqrt(ms + RMS_EPS), 0.0)
        o_ref[...] = ((o * inv * g_ref[...]) * (1.0 - lam_init)).astype(o_ref.dtype)

    _dense_maps(n_maps, defer, score, _value_block(v_ref, kt_ref.shape[1]), m_ref, l_ref, acc_ref, s_ref, finish)


def _diff_call(q, kt, v, lam, g_tiled, lam_init, bsz, t_len, tq, tk):
    n = q.shape[0]
    n_maps = 2 * N_HEADS
    defer = t_len // tk > 2
    grid, in_specs, out_spec = _dense_specs(bsz, t_len, tq, tk, MIX_COLS, MIX_COLS)
    in_specs += [pl.BlockSpec((1, 1), lambda b, i, j: (0, 0)), pl.BlockSpec((1, MIX_COLS), lambda b, i, j: (0, 0))]
    return pl.pallas_call(
        functools.partial(_diff_kernel, lam_init, defer), grid=grid, in_specs=in_specs, out_specs=out_spec,
        out_shape=jax.ShapeDtypeStruct((n, MIX_COLS), BF16),
        scratch_shapes=[pltpu.VMEM((n_maps, tq, MIX_COLS), BF16)] + _dense_scratch(n_maps, tq, tk, defer),
        compiler_params=_cparams(("arbitrary", "arbitrary", "arbitrary")), name="attn_diff",
    )(q, kt, v, lam, g_tiled)


def _window_heads(q, kw, vw, add, mult, o_ref):
    head = _head_of_lane(q.shape)
    out = jnp.zeros(q.shape, F32)
    for hd in range(N_HEADS):
        qh = jnp.where(head == hd, q, jnp.zeros_like(q))
        s = _nt_dot(qh, kw)
        if add is not None:
            s = s + add(hd)
        if mult is not None:
            s = jnp.where(mult > 0.0, s, NEG_BIG)
        m = jnp.max(s, axis=1, keepdims=True)
        p = jnp.exp2(s - m)
        if mult is not None:
            p = p * mult
        l = jnp.sum(p, axis=1, keepdims=True)
        o = jnp.dot(p.astype(BF16), vw, preferred_element_type=F32)
        out = out + jnp.where(head == hd, o * (1.0 / l), 0.0)
    o_ref[...] = out.astype(o_ref.dtype)


def _nbr_kernel(nb, q_ref, k_ref, v_ref, bias_ref, o_ref):
    i = pl.program_id(1)
    start = pl.multiple_of(jnp.clip(i - 1, 0, nb - B_WIN // QBLK) * QBLK, QBLK)
    kw = k_ref[pl.ds(start, B_WIN), :]
    vw = v_ref[pl.ds(start, B_WIN), :]
    _window_heads(q_ref[...], kw, vw, lambda hd: bias_ref[0, hd], None, o_ref)


def _nbr_block_type(i, nb):
    return jnp.where(i < 1, 0, jnp.where(i > nb - 2, 2, 1))


def _nbr_call(q, k, v, bias, bsz, t_len):
    n = q.shape[0]
    nb = t_len // QBLK
    return pl.pallas_call(
        functools.partial(_nbr_kernel, nb), grid=(bsz, nb),
        in_specs=[pl.BlockSpec((QBLK, MIX_COLS), lambda b, i: (b * nb + i, 0)),
                  pl.BlockSpec((t_len, MIX_COLS), lambda b, i: (b, 0)),
                  pl.BlockSpec((t_len, MIX_COLS), lambda b, i: (b, 0)),
                  pl.BlockSpec((1, N_HEADS, QBLK, B_WIN), lambda b, i: (_nbr_block_type(i, nb), 0, 0, 0))],
        out_specs=pl.BlockSpec((QBLK, MIX_COLS), lambda b, i: (b * nb + i, 0)),
        out_shape=jax.ShapeDtypeStruct((n, MIX_COLS), BF16),
        compiler_params=_cparams(("parallel", "arbitrary")), name="attn_nbr",
    )(q, k, v, bias)


def _nbr_bias_tables(rpb, t_len):
    rows = t_len // GRID_W
    nb = t_len // QBLK
    kr = min(NA_ROWS, rows)
    reps = (0, 1, nb - 1)
    n_dr, n_dc = 2 * NA_ROWS - 1, 2 * NA_COLS - 1
    q_rows, w_rows = QBLK // GRID_W, B_WIN // GRID_W
    qc = np.arange(GRID_W)[:, None]
    kc = np.arange(GRID_W)[None, :]
    cs = np.clip(qc - NA_COLS // 2, 0, GRID_W - NA_COLS)
    col_ok = (kc >= cs) & (kc < cs + NA_COLS)
    sel_c = ((kc - qc + NA_COLS - 1)[None] == np.arange(n_dc)[:, None, None]) & col_ok[None]
    sel_r = np.zeros((len(reps), q_rows, w_rows, n_dr), np.float32)
    row_ok = np.zeros((len(reps), q_rows, w_rows), bool)
    for ti, b in enumerate(reps):
        start_row = int(np.clip(b - 1, 0, nb - B_WIN // QBLK)) * q_rows
        for qr in range(q_rows):
            r = b * q_rows + qr
            rs = int(np.clip(r - kr // 2, 0, rows - kr))
            for wr in range(w_rows):
                key_r = start_row + wr
                if rs <= key_r < rs + kr:
                    row_ok[ti, qr, wr] = True
                    sel_r[ti, qr, wr, key_r - r + NA_ROWS - 1] = 1.0
    hp = lax.Precision.HIGHEST
    by_col = jnp.einsum("hrd,dqk->hrqk", rpb.astype(F32), jnp.asarray(sel_c, F32), precision=hp)
    dense = jnp.einsum("tawr,hrqk->thaqwk", jnp.asarray(sel_r), by_col, precision=hp) * LOG2E
    ok = row_ok[:, None, :, None, :, None] & col_ok[None, None, None, :, None, :]
    return jnp.where(ok, dense, NEG_BIG).reshape(len(reps), N_HEADS, QBLK, B_WIN)


def _dil_kernel(t_len, tq, q_ref, k_ref, v_ref, mask_ref, o_ref):
    i = pl.program_id(1)
    w = tq + 2 * D_REACH
    start = pl.multiple_of(jnp.clip(i * tq - D_REACH, 0, t_len - w), tq)
    kw = k_ref[pl.ds(start, w), :]
    vw = v_ref[pl.ds(start, w), :]
    _window_heads(q_ref[...], kw, vw, lambda hd: mask_ref[0], None, o_ref)


def _dil_masks(tq):
    w = tq + 2 * D_REACH
    n_place = 2 * D_REACH // tq + 1
    place = jnp.arange(n_place, dtype=jnp.int32)[:, None, None] * tq
    d = (lax.broadcasted_iota(jnp.int32, (n_place, tq, w), 2) - place
         - lax.broadcasted_iota(jnp.int32, (n_place, tq, w), 1))
    ad = jnp.abs(d)
    count = ((ad <= 64).astype(F32)
             + (((d & 3) == 0) & (ad <= 256)).astype(F32)
             + (((d & 15) == 0) & (ad <= D_REACH)).astype(F32))
    return jnp.where(count > 0.0, jnp.log2(jnp.maximum(count, 1.0)), NEG_BIG)


def _dil_call(q, k, v, masks, bsz, t_len, tq):
    n = q.shape[0]
    nq = t_len // tq
    w = tq + 2 * D_REACH
    n_place = masks.shape[0]
    half = D_REACH // tq

    def placement(i):
        return jnp.where(i < half, i, jnp.where(i > nq - 1 - half, i - nq + n_place, half))

    whole = lambda: pl.BlockSpec((t_len, MIX_COLS), lambda b, i: (b, 0), pipeline_mode=pl.Buffered(1))
    return pl.pallas_call(
        functools.partial(_dil_kernel, t_len, tq), grid=(bsz, nq),
        in_specs=[pl.BlockSpec((tq, MIX_COLS), lambda b, i: (b * nq + i, 0)), whole(), whole(),
                  pl.BlockSpec((1, tq, w), lambda b, i: (placement(i), 0, 0))],
        out_specs=pl.BlockSpec((tq, MIX_COLS), lambda b, i: (b * nq + i, 0)),
        out_shape=jax.ShapeDtypeStruct((n, MIX_COLS), BF16),
        compiler_params=_cparams(("arbitrary", "arbitrary")), name="attn_dil",
    )(q, k, v, masks)


def _out_kernel(oa_ref, ob_ref, oc_ref, od_ref, x_ref, wo_ref, g2_ref, wr_ref, x2_ref, hn_ref, aff_ref):
    acc = x_ref[...]
    for mi, o_ref in enumerate((oa_ref, ob_ref, oc_ref, od_ref)):
        acc = acc + jnp.dot(o_ref[...], wo_ref[MIX_COLS * mi:MIX_COLS * (mi + 1), :], preferred_element_type=F32)
    x2_ref[...] = acc
    hn = _rms(acc) * g2_ref[...]
    hn_hi = hn.astype(BF16)
    hn_ref[...] = hn_hi
    hn_lo = (hn - hn_hi.astype(F32)).astype(BF16)
    logits = (jnp.dot(hn_hi, wr_ref[0], preferred_element_type=F32)
              + jnp.dot(hn_lo, wr_ref[0], preferred_element_type=F32)
              + jnp.dot(hn_hi, wr_ref[1], preferred_element_type=F32))
    lane = lax.broadcasted_iota(jnp.int32, logits.shape, 1)
    logits = jnp.where(lane < N_EXPERTS, logits, -jnp.inf)
    m = jnp.max(logits, axis=1, keepdims=True)
    e = jnp.exp(logits - m)
    aff_ref[...] = e / jnp.sum(e, axis=1, keepdims=True)


def _out_call(oa, ob, oc, od, x2d, wo, g2, wr, tm):
    n = x2d.shape[0]
    full = lambda a: pl.BlockSpec(a.shape, lambda i: (0,) * a.ndim)
    row = lambda w: pl.BlockSpec((tm, w), lambda i: (i, 0))
    return pl.pallas_call(
        _out_kernel, grid=(n // tm,),
        in_specs=[row(MIX_COLS)] * 4 + [row(D_MODEL), full(wo), full(g2), full(wr)],
        out_specs=(row(D_MODEL), row(D_MODEL), row(LANES)),
        out_shape=(jax.ShapeDtypeStruct((n, D_MODEL), F32), jax.ShapeDtypeStruct((n, D_MODEL), BF16),
                   jax.ShapeDtypeStruct((n, LANES), F32)),
        compiler_params=_cparams(("parallel",)), name="proj_out",
    )(oa, ob, oc, od, x2d, wo, g2, wr)


FLAG_FIRST, FLAG_LAST, FLAG_SUB = 1, 2, 4
Y_BLK = 128
GATHER_FANIN = 4
COMBINE_FANIN = 8


def _count_le(sorted_vals, queries):
    return jnp.sum((sorted_vals[None, :] <= queries[:, None]).astype(jnp.int32), axis=1)


def _step_lists(cnt, fanin, n_steps):
    steps = jnp.maximum((cnt + fanin - 1) // fanin, 1)
    ends = jnp.cumsum(steps)
    w = jnp.arange(n_steps, dtype=jnp.int32)
    grp = jnp.minimum(_count_le(ends, w), cnt.shape[0] - 1)
    per_grp = jnp.stack([ends - steps, steps, cnt], axis=1)[grp]
    rank = w - per_grp[:, 0]
    valid = w < ends[-1]
    item0 = rank * fanin
    flags = (jnp.where(valid & (rank == 0), FLAG_FIRST, 0)
             | jnp.where(valid & (rank == per_grp[:, 1] - 1), FLAG_LAST, 0))
    return grp, item0, flags, jnp.where(valid, per_grp[:, 2] - item0, 0)


def _with_live_flags(flags, remaining, slot_ranks):
    for k, r in enumerate(slot_ranks):
        flags = flags | jnp.where(r < remaining, FLAG_SUB << k, 0)
    return flags.astype(jnp.int32)


def _route_tables(aff, n, cap, ts, tc, tt):
    gate, idx = lax.top_k(aff[:, :N_EXPERTS].T, cap)
    idx, gate = lax.sort((idx, gate), dimension=1, num_keys=1)
    idx = idx.astype(jnp.int32)

    tiles_per_e = cap // ts
    n_tiles = N_EXPERTS * tiles_per_e
    n_chunks = n // tc
    c0 = (idx[:, ::ts] // tc).reshape(n_tiles)
    c1 = (idx[:, ts - 1::ts] // tc).reshape(n_tiles)
    n_gs = (N_EXPERTS * n_chunks + n_tiles) // GATHER_FANIN + n_tiles
    g_tile, g_item0, g_flags, g_left = _step_lists(c1 - c0 + 1, GATHER_FANIN, n_gs)
    g_first = c0[g_tile] + g_item0
    g_ranks = [(k - g_first) % GATHER_FANIN for k in range(GATHER_FANIN)]
    g_flags = _with_live_flags(g_flags, g_left, g_ranks)
    g_chunks = jnp.concatenate([jnp.minimum(g_first + r, n_chunks - 1) for r in g_ranks])

    n_tt = n // tt
    blk_per_e = cap // Y_BLK
    bounds = jnp.arange(n_tt + 1, dtype=jnp.int32) * tt
    pos = jnp.sum((idx[:, None, :] < bounds[None, :, None]).astype(jnp.int32), axis=2)
    lo, hi = pos[:, :-1], pos[:, 1:]
    b0 = jnp.minimum(lo // Y_BLK, blk_per_e - 1)
    b1 = jnp.maximum((hi - 1) // Y_BLK, b0)
    pair_cnt = jnp.where(hi > lo, b1 - b0 + 1, 0).T.reshape(-1)
    pair_b0 = (b0 + (jnp.arange(N_EXPERTS, dtype=jnp.int32) * blk_per_e)[:, None]).T.reshape(-1)
    n_items = N_EXPERTS * blk_per_e + N_EXPERTS * n_tt
    item_ends = jnp.cumsum(pair_cnt)
    it = jnp.arange(n_items, dtype=jnp.int32)
    pair = jnp.minimum(_count_le(item_ends, it), pair_cnt.shape[0] - 1)
    item_blk = jnp.minimum(pair_b0[pair] + it - (item_ends - pair_cnt)[pair], N_EXPERTS * blk_per_e - 1)
    tile_cnt = jnp.sum(pair_cnt.reshape(n_tt, N_EXPERTS), axis=1)
    tile_item0 = jnp.cumsum(tile_cnt) - tile_cnt
    n_cs = n_items // COMBINE_FANIN + n_tt
    c_tile, c_item0, c_flags, c_left = _step_lists(tile_cnt, COMBINE_FANIN, n_cs)
    c_flags = _with_live_flags(c_flags, c_left, list(range(COMBINE_FANIN)))
    c_first = tile_item0[c_tile] + c_item0
    c_blks = jnp.concatenate([item_blk[jnp.minimum(c_first + k, n_items - 1)] for k in range(COMBINE_FANIN)])
    return idx, gate, (g_tile, g_chunks, g_flags), (c_tile, c_blks, c_flags)


def _ffn_kernel(tc, tile_ref, chunk_ref, flag_ref, tok_ref, gate_ref, *refs):
    hn_refs = refs[:GATHER_FANIN]
    wg_ref, wu_ref, wd_ref, y_ref, xacc = refs[GATHER_FANIN:]
    w = pl.program_id(0)
    n_steps = pl.num_programs(0)
    flags = flag_ref[w]

    @pl.when((flags & FLAG_FIRST) != 0)
    def _():
        xacc[...] = jnp.zeros(xacc.shape, F32)

    for k in range(GATHER_FANIN):
        @pl.when((flags & (FLAG_SUB << k)) != 0)
        def _(k=k):
            ts = tok_ref.shape[0]
            token = chunk_ref[k * n_steps + w] * tc + lax.broadcasted_iota(jnp.int32, (ts, tc), 1)
            onehot = jnp.where(tok_ref[...] == token, 1.0, 0.0).astype(BF16)
            xacc[...] += jnp.dot(onehot, hn_refs[k][...], preferred_element_type=F32)

    @pl.when((flags & FLAG_LAST) != 0)
    def _():
        xg = xacc[...].astype(BF16)
        a = jnp.dot(xg, wg_ref[0, 0], preferred_element_type=F32)
        b = jnp.dot(xg, wu_ref[0, 0], preferred_element_type=F32)
        hid = (a * jax.nn.sigmoid(a) * b).astype(BF16)
        y_ref[...] = (jnp.dot(hid, wd_ref[0, 0], preferred_element_type=F32) * gate_ref[...]).astype(BF16)


def _ffn_call(steps, idx, gate, hn, w_gate, w_up, w_down, layer, ts, tc):
    n_exp, cap = idx.shape
    tiles_per_e = cap // ts
    tile, chunks, flags = steps
    n_steps = tile.shape[0]
    tok = idx.reshape(n_exp * cap, 1)
    gate2 = gate.reshape(n_exp * cap, 1)
    slot = lambda wd: pl.BlockSpec((ts, wd), lambda w, tile, chunks, flags: (tile[w], 0))
    chunk = lambda k: pl.BlockSpec((tc, D_MODEL), lambda w, tile, chunks, flags: (chunks[k * n_steps + w], 0))
    wspec = pl.BlockSpec((1, 1, D_MODEL, D_MODEL), lambda w, tile, chunks, flags: (layer, tile[w] // tiles_per_e, 0, 0))
    grid_spec = pltpu.PrefetchScalarGridSpec(
        num_scalar_prefetch=3, grid=(n_steps,),
        in_specs=[slot(1), slot(1)] + [chunk(k) for k in range(GATHER_FANIN)] + [wspec, wspec, wspec],
        out_specs=slot(D_MODEL),
        scratch_shapes=[pltpu.VMEM((ts, D_MODEL), F32)])
    return pl.pallas_call(
        functools.partial(_ffn_kernel, tc), grid_spec=grid_spec,
        out_shape=jax.ShapeDtypeStruct((n_exp * cap, D_MODEL), BF16),
        compiler_params=_cparams(("arbitrary",)), name="expert_ffn",
    )(tile, chunks, flags, tok, gate2, *([hn] * GATHER_FANIN), w_gate, w_up, w_down)


def _combine_kernel(tt, final, tile_ref, blk_ref, flag_ref, *refs):
    tok_refs = refs[:COMBINE_FANIN]
    y_refs = refs[COMBINE_FANIN:2 * COMBINE_FANIN]
    x_ref, g_ref, o_ref = refs[2 * COMBINE_FANIN:]
    w = pl.program_id(0)
    flags = flag_ref[w]

    @pl.when((flags & FLAG_FIRST) != 0)
    def _():
        o_ref[...] = x_ref[...]

    @pl.when((flags & FLAG_SUB) != 0)
    def _():
        token = tile_ref[w] * tt + lax.broadcasted_iota(jnp.int32, (tt, Y_BLK), 0)
        hots = []
        for k in range(COMBINE_FANIN):
            live = (flags & (FLAG_SUB << k)) != 0
            tok = jnp.where(live, tok_refs[k][0], -1)
            hots.append(jnp.where(token == tok, 1.0, 0.0).astype(BF16))
        onehot = jnp.concatenate(hots, axis=1)
        ycat = jnp.concatenate([y_refs[k][...] for k in range(COMBINE_FANIN)], axis=0)
        o_ref[...] += jnp.dot(onehot, ycat, preferred_element_type=F32)

    if final:
        @pl.when((flags & FLAG_LAST) != 0)
        def _():
            o_ref[...] = _rms(o_ref[...]) * g_ref[...]


def _combine_call(steps, idx, y, x2, g_final, tt, final):
    n = x2.shape[0]
    tile, blks, flags = steps
    n_steps = tile.shape[0]
    tok = idx.reshape(-1, 1, Y_BLK)
    tspec = lambda k: pl.BlockSpec((1, 1, Y_BLK), lambda w, tile, blks, flags: (blks[k * n_steps + w], 0, 0))
    yspec = lambda k: pl.BlockSpec((Y_BLK, D_MODEL), lambda w, tile, blks, flags: (blks[k * n_steps + w], 0))
    xspec = pl.BlockSpec((tt, D_MODEL), lambda w, tile, blks, flags: (tile[w], 0))
    fan = range(COMBINE_FANIN)
    grid_spec = pltpu.PrefetchScalarGridSpec(
        num_scalar_prefetch=3, grid=(n_steps,),
        in_specs=[tspec(k) for k in fan] + [yspec(k) for k in fan]
        + [xspec, pl.BlockSpec((1, D_MODEL), lambda w, tile, blks, flags: (0, 0))],
        out_specs=xspec)
    return pl.pallas_call(
        functools.partial(_combine_kernel, tt, final), grid_spec=grid_spec,
        out_shape=jax.ShapeDtypeStruct((n, D_MODEL), F32),
        compiler_params=_cparams(("arbitrary",)), name="expert_combine",
    )(tile, blks, flags, *([tok] * COMBINE_FANIN), *([y] * COMBINE_FANIN), x2, g_final)


def _rot_cols(w, d):
    k, c = w.shape
    half = d // 2
    wg = w.reshape(k, c // d, 2, half)
    return jnp.concatenate([-wg[:, :, 1], wg[:, :, 0]], axis=2).reshape(k, c)


def _rope_tables(t_len):
    pos = jnp.arange(t_len, dtype=F32)

    def cs(d):
        half = d // 2
        inv = ROPE_THETA ** (-jnp.arange(half, dtype=F32) / half)
        ang = pos[:, None] * inv[None, :]
        return (jnp.concatenate([jnp.cos(ang)] * 2, axis=1), jnp.concatenate([jnp.sin(ang)] * 2, axis=1))

    c32, s32 = cs(A_ROPE)
    c64, s64 = cs(HEAD_DIM)
    ones = jnp.ones((t_len, A_NOPE), F32)
    zeros = jnp.zeros((t_len, A_NOPE), F32)
    pad = jnp.zeros((t_len, A_PAD - A_NOPE - A_ROPE), F32)
    tab = jnp.concatenate([ones, c32, pad, zeros, s32, pad,
                           jnp.tile(c32, (1, LANES // A_ROPE)), jnp.tile(s32, (1, LANES // A_ROPE)),
                           jnp.tile(c64, (1, LANES // HEAD_DIM)), jnp.tile(s64, (1, LANES // HEAD_DIM))], axis=1)
    tabt = jnp.concatenate([c32.T, s32.T], axis=0)
    return tab, tabt


def _layer_weights(l, w_in, a_w_uq, a_w_ukv, w_out, w_router):
    pts = np.cumsum(PROJ_SIZES)[:-1]
    (a_cq, a_ckv, a_kr, b_q, b_k, b_v, c_q, c_k, c_v, d_q, d_k, d_v) = jnp.split(w_in[l], pts, axis=1)
    wm = jnp.concatenate([a_cq, a_ckv, b_q, b_k, b_v, c_q, _rot_cols(c_q, C_DIM), c_v,
                          d_q, _rot_cols(d_q, HEAD_DIM), d_k, _rot_cols(d_k, HEAD_DIM), d_v], axis=1).astype(BF16)
    wt = jnp.concatenate([c_k, _rot_cols(c_k, C_DIM), a_kr, _rot_cols(a_kr, A_ROPE)], axis=1).T.astype(BF16)
    uq = a_w_uq[l].reshape(A_Q_LORA, N_HEADS, A_NOPE + A_ROPE)
    zpad = jnp.zeros((A_Q_LORA, N_HEADS, A_PAD - A_NOPE - A_ROPE), F32)
    uq_rope = uq[:, :, A_NOPE:]
    uq_rot = _rot_cols(uq_rope.reshape(A_Q_LORA, N_HEADS * A_ROPE), A_ROPE).reshape(A_Q_LORA, N_HEADS, A_ROPE)
    plain = jnp.concatenate([uq, zpad], axis=2).reshape(A_Q_LORA, N_HEADS * A_PAD)
    rot = jnp.concatenate([jnp.zeros_like(uq[:, :, :A_NOPE]), uq_rot, zpad], axis=2).reshape(A_Q_LORA, N_HEADS * A_PAD)
    wuq = jnp.concatenate([plain, rot], axis=1).astype(BF16)
    ukv = a_w_ukv[l].reshape(A_KV_LORA, N_HEADS, 2 * HEAD_DIM)
    wukt = ukv[:, :, :A_NOPE].reshape(A_KV_LORA, MIX_COLS).T.astype(BF16)
    wuv = ukv[:, :, A_NOPE:].reshape(A_KV_LORA, MIX_COLS).astype(BF16)
    wo = w_out[l].astype(BF16)
    wr = jnp.concatenate([w_router[l], jnp.zeros((D_MODEL, LANES - N_EXPERTS), F32)], axis=1)
    wr_hi = wr.astype(BF16)
    wr = jnp.stack([wr_hi, (wr - wr_hi.astype(F32)).astype(BF16)])
    return wm, wt, wuq, wukt, wuv, wo, wr


def _trunk(x, norm1_g, w_in, a_qnorm_g, a_w_uq, a_kvnorm_g, a_w_ukv, b_rpb, c_lambda, c_subln_g, w_out,
           norm2_g, w_router, w_gate, w_up, w_down, final_g):
    bsz, t_len, _ = x.shape
    n = bsz * t_len
    tm = 512
    tq = 512
    tk_mla, tk_diff = min(4096, t_len), 2048
    tq_dil = 256
    ts, tc, tt = 256, 1024, 512
    depth = w_in.shape[0]
    cap = CAP_FACTOR * n // N_EXPERTS
    assert t_len % tk_mla == 0 and t_len % tk_diff == 0 and t_len >= tq_dil + 2 * D_REACH and t_len // QBLK >= 3
    assert cap % ts == 0 and n % tc == 0 and n % tt == 0
    tab, tabt = _rope_tables(t_len)
    dil_masks = _dil_masks(tq_dil)
    x2d = x.reshape(n, D_MODEL)
    g_final = final_g.reshape(1, D_MODEL)
    for l in range(depth):
        lam_init = 0.8 - 0.6 * math.exp(-0.3 * l)
        wm, wt, wuq, wukt, wuv, wo, wr = _layer_weights(l, w_in, a_w_uq, a_w_ukv, w_out, w_router)
        row = lambda g: g[l].reshape(1, -1)
        (qa, kat, va, qb, kb, vb, qc, kct, vc, qd, kd, vd) = _proj_call(
            x2d, t_len, row(norm1_g), wm, wt, row(a_qnorm_g), wuq, row(a_kvnorm_g), wukt, wuv, tab, tabt, tm)
        oa = _mla_call(qa, kat, va, bsz, t_len, tq, tk_mla)
        ob = _nbr_call(qb, kb, vb, _nbr_bias_tables(b_rpb[l], t_len), bsz, t_len)
        lp = c_lambda[l].astype(F32)
        lam = (jnp.exp(jnp.sum(lp[0] * lp[1])) - jnp.exp(jnp.sum(lp[2] * lp[3])) + lam_init).reshape(1, 1)
        g_sub = jnp.tile(c_subln_g[l], N_HEADS).reshape(1, MIX_COLS)
        oc = _diff_call(qc, kct, vc, lam, g_sub, lam_init, bsz, t_len, tq, tk_diff)
        od = _dil_call(qd, kd, vd, dil_masks, bsz, t_len, tq_dil)
        x2, hn, aff = _out_call(oa, ob, oc, od, x2d, wo, row(norm2_g), wr, tm)
        idx, gate, g_items, c_items = _route_tables(aff, n, cap, ts, tc, tt)
        y = _ffn_call(g_items, idx, gate, hn, w_gate, w_up, w_down, l, ts, tc)
        x2d = _combine_call(c_items, idx, y, x2, g_final, tt, l == depth - 1)
    return x2d.reshape(bsz, t_len, D_MODEL)


def kernel(x_prompt, x_sample, norm1_g, w_in, a_qnorm_g, a_w_uq, a_kvnorm_g, a_w_ukv, b_rpb, c_lambda, c_subln_g,
           w_out, norm2_g, w_router, w_gate, w_up, w_down, final_g):
    params = (norm1_g, w_in, a_qnorm_g, a_w_uq, a_kvnorm_g, a_w_ukv, b_rpb, c_lambda, c_subln_g, w_out,
              norm2_g, w_router, w_gate.astype(BF16), w_up.astype(BF16), w_down.astype(BF16), final_g)
    return (_trunk(x_prompt, *params), _trunk(x_sample, *params))
```

```python
import functools
import math

import numpy as np
import jax
import jax.numpy as jnp
from jax import lax
from jax.experimental import pallas as pl
from jax.experimental.pallas import tpu as pltpu

F32 = jnp.float32
BF16 = jnp.bfloat16

D_MODEL = 1024
N_HEADS = 4
HEAD_DIM = 64
MIX_COLS = N_HEADS * HEAD_DIM
A_NOPE = 64
A_ROPE = 32
A_PAD = 128
A_Q_LORA = 256
A_KV_LORA = 128
C_DIM = 32
GRID_W = 64
NA_ROWS = 8
NA_COLS = 16
QBLK = 256
B_WIN = 3 * QBLK
D_REACH = 1024
N_EXPERTS = 16
CAP_FACTOR = 2
ROPE_THETA = 10000.0
RMS_EPS = 1e-6
NEG_BIG = -1e30
LOG2E = math.log2(math.e)
LANES = 128
V7X_VMEM_LIMIT = 56 * 1024 * 1024

PROJ_SIZES = (256, 128, 32, 256, 256, 256, 256, 256, 256, 256, 256, 256)

_M_ACQ, _M_ACKV, _M_BQ, _M_BK, _M_BV = 0, 256, 384, 640, 896
_M_CQ, _M_CQR, _M_CV = 1152, 1408, 1664
_M_DQ, _M_DQR, _M_DK, _M_DKR, _M_DV = 1920, 2176, 2432, 2688, 2944
_M_COLS = 3200
_T_CK, _T_CKR, _T_AKR, _T_AKRR, _T_ROWS = 0, 256, 512, 544, 576


def _cparams(sem):
    return pltpu.CompilerParams(dimension_semantics=sem, vmem_limit_bytes=V7X_VMEM_LIMIT)


def _rms(x):
    return x * lax.rsqrt(jnp.mean(x * x, axis=-1, keepdims=True) + RMS_EPS)


def _nt_dot(a, b):
    return lax.dot_general(a, b, (((1,), (1,)), ((), ())), preferred_element_type=F32)


def _head_of_lane(shape):
    return lax.broadcasted_iota(jnp.int32, shape, len(shape) - 1) // HEAD_DIM


def _proj_kernel(x_ref, g1_ref, wm_ref, wt_ref, gq_ref, wuq_ref, gkv_ref, wukt_ref, wuv_ref,
                 tab_ref, tabt_ref,
                 qa_ref, kat_ref, va_ref, qb_ref, kb_ref, vb_ref, qc_ref, kct_ref, vc_ref,
                 qd_ref, kd_ref, vd_ref):
    x = x_ref[...]
    h = (_rms(x) * g1_ref[...]).astype(BF16)
    p = jnp.dot(h, wm_ref[...], preferred_element_type=F32)
    pt = _nt_dot(wt_ref[...], h)
    tab = tab_ref[...]
    cos_a, sin_a, cos_c, sin_c, cos_d, sin_d = (tab[:, LANES * i:LANES * (i + 1)] for i in range(6))
    tabt = tabt_ref[...]
    cos_t, sin_t = tabt[0:A_ROPE], tabt[A_ROPE:2 * A_ROPE]

    scale_a = LOG2E * (A_NOPE + A_ROPE) ** -0.5
    latq = (_rms(p[:, _M_ACQ:_M_ACQ + A_Q_LORA]) * gq_ref[...]).astype(BF16)
    qa2 = jnp.dot(latq, wuq_ref[...], preferred_element_type=F32)
    for hd in range(N_HEADS):
        lo = A_PAD * hd
        blk = qa2[:, lo:lo + A_PAD] * cos_a + qa2[:, N_HEADS * A_PAD + lo:N_HEADS * A_PAD + lo + A_PAD] * sin_a
        qa_ref[:, lo:lo + A_PAD] = (blk * scale_a).astype(BF16)
    latkv = (_rms(p[:, _M_ACKV:_M_ACKV + A_KV_LORA]) * gkv_ref[...]).astype(BF16)
    va_ref[...] = jnp.dot(latkv, wuv_ref[...], preferred_element_type=F32).astype(BF16)
    knt = _nt_dot(wukt_ref[...], latkv)
    krt = (pt[_T_AKR:_T_AKR + A_ROPE] * cos_t + pt[_T_AKRR:_T_AKRR + A_ROPE] * sin_t).astype(BF16)
    tm = x.shape[0]
    for hd in range(N_HEADS):
        lo = A_PAD * hd
        kat_ref[lo:lo + A_NOPE, :] = knt[A_NOPE * hd:A_NOPE * (hd + 1)].astype(BF16)
        kat_ref[lo + A_NOPE:lo + A_NOPE + A_ROPE, :] = krt
        kat_ref[lo + A_NOPE + A_ROPE:lo + A_PAD, :] = jnp.zeros((A_PAD - A_NOPE - A_ROPE, tm), BF16)

    qb_ref[...] = (p[:, _M_BQ:_M_BQ + MIX_COLS] * (LOG2E * HEAD_DIM ** -0.5)).astype(BF16)
    kb_ref[...] = p[:, _M_BK:_M_BK + MIX_COLS].astype(BF16)
    vb_ref[...] = p[:, _M_BV:_M_BV + MIX_COLS].astype(BF16)

    scale_c = LOG2E * C_DIM ** -0.5
    for j in range(MIX_COLS // LANES):
        lo = LANES * j
        blk = p[:, _M_CQ + lo:_M_CQ + lo + LANES] * cos_c + p[:, _M_CQR + lo:_M_CQR + lo + LANES] * sin_c
        qc_ref[:, lo:lo + LANES] = (blk * scale_c).astype(BF16)
    reps = MIX_COLS // A_ROPE
    cos_ct = jnp.concatenate([cos_t] * reps, axis=0)
    sin_ct = jnp.concatenate([sin_t] * reps, axis=0)
    kct_ref[...] = (pt[_T_CK:_T_CK + MIX_COLS] * cos_ct + pt[_T_CKR:_T_CKR + MIX_COLS] * sin_ct).astype(BF16)
    vc_ref[...] = p[:, _M_CV:_M_CV + MIX_COLS].astype(BF16)

    for j in range(MIX_COLS // LANES):
        lo = LANES * j
        qblk = p[:, _M_DQ + lo:_M_DQ + lo + LANES] * cos_d + p[:, _M_DQR + lo:_M_DQR + lo + LANES] * sin_d
        qd_ref[:, lo:lo + LANES] = (qblk * (LOG2E * HEAD_DIM ** -0.5)).astype(BF16)
        kblk = p[:, _M_DK + lo:_M_DK + lo + LANES] * cos_d + p[:, _M_DKR + lo:_M_DKR + lo + LANES] * sin_d
        kd_ref[:, lo:lo + LANES] = kblk.astype(BF16)
    vd_ref[...] = p[:, _M_DV:_M_DV + MIX_COLS].astype(BF16)


def _proj_call(x2d, t_len, g1, wm, wt, gq, wuq, gkv, wukt, wuv, tab, tabt, tm):
    n = x2d.shape[0]
    nt = t_len // tm
    full = lambda a: pl.BlockSpec(a.shape, lambda i: (0,) * a.ndim)
    row = lambda w: pl.BlockSpec((tm, w), lambda i: (i, 0))
    col = lambda r: pl.BlockSpec((r, tm), lambda i: (0, i))
    tok = lambda w: jax.ShapeDtypeStruct((n, w), BF16)
    out_shape = (tok(N_HEADS * A_PAD), jax.ShapeDtypeStruct((N_HEADS * A_PAD, n), BF16), tok(MIX_COLS),
                 tok(MIX_COLS), tok(MIX_COLS), tok(MIX_COLS),
                 tok(MIX_COLS), jax.ShapeDtypeStruct((MIX_COLS, n), BF16), tok(MIX_COLS),
                 tok(MIX_COLS), tok(MIX_COLS), tok(MIX_COLS))
    out_specs = (row(N_HEADS * A_PAD), col(N_HEADS * A_PAD), row(MIX_COLS),
                 row(MIX_COLS), row(MIX_COLS), row(MIX_COLS),
                 row(MIX_COLS), col(MIX_COLS), row(MIX_COLS),
                 row(MIX_COLS), row(MIX_COLS), row(MIX_COLS))
    in_specs = [row(D_MODEL), full(g1), full(wm), full(wt), full(gq), full(wuq), full(gkv), full(wukt), full(wuv),
                pl.BlockSpec((tm, tab.shape[1]), lambda i: (i % nt, 0)),
                pl.BlockSpec((tabt.shape[0], tm), lambda i: (0, i % nt))]
    return pl.pallas_call(
        _proj_kernel, grid=(n // tm,), in_specs=in_specs, out_specs=out_specs, out_shape=out_shape,
        compiler_params=_cparams(("parallel",)), name="proj_in",
    )(x2d, g1, wm, wt, gq, wuq, gkv, wukt, wuv, tab, tabt)


def _row_total(l_lanes):
    return jnp.sum(l_lanes, axis=1, keepdims=True)


def _online_step(s, v, m_ref, l_ref, acc_ref, idx, first, keep=None):
    m_prev = jnp.where(first, -jnp.inf, m_ref[idx])
    l_prev = jnp.where(first, 0.0, l_ref[idx])
    acc_prev = jnp.where(first, 0.0, acc_ref[idx])
    m_new = jnp.maximum(m_prev, jnp.max(s, axis=1, keepdims=True))
    alpha = jnp.exp2(m_prev - m_new)
    p = jnp.exp2(s - m_new)
    part = p[:, 0:LANES]
    for c in range(1, s.shape[1] // LANES):
        part = part + p[:, LANES * c:LANES * (c + 1)]
    l_new = alpha * l_prev + part
    acc_new = alpha * acc_prev + jnp.dot(p.astype(BF16), v, preferred_element_type=F32)
    if keep is not None:
        l_new, acc_new = jnp.where(keep, l_new, 0.0), jnp.where(keep, acc_new, 0.0)
        m_new = jnp.where(keep, m_new, -jnp.inf)
    l_ref[idx] = l_new
    acc_ref[idx] = acc_new
    m_ref[idx] = m_new


def _zero_state_once(refs):
    @pl.when((pl.program_id(0) == 0) & (pl.program_id(1) == 0) & (pl.program_id(2) == 0))
    def _():
        for r in refs:
            r[...] = jnp.zeros(r.shape, r.dtype)


def _dense_maps(n_maps, defer, score, v_block, m_ref, l_ref, acc_ref, s_ref, finish):
    j = pl.program_id(2)
    first = j == 0
    is_last = j == pl.num_programs(2) - 1
    _zero_state_once([m_ref, l_ref, acc_ref] + ([s_ref] if defer else []))
    if defer:
        keep = jnp.logical_not(first)
        _online_step(s_ref[...], v_block(jnp.maximum(j - 1, 0)), m_ref, l_ref, acc_ref, n_maps - 1, first, keep)
    v = v_block(j)
    for mi in range(n_maps - 1 if defer else n_maps):
        _online_step(score(mi), v, m_ref, l_ref, acc_ref, mi, first)
    if defer:
        s_ref[...] = score(n_maps - 1)

    @pl.when(is_last)
    def _():
        if defer:
            _online_step(s_ref[...], v_block(j), m_ref, l_ref, acc_ref, n_maps - 1, False)
        finish()


def _value_block(v_ref, tk):
    return lambda j: v_ref[pl.ds(pl.multiple_of(j * tk, tk), tk), :]


def _mla_kernel(defer, q_ref, kt_ref, v_ref, o_ref, m_ref, l_ref, acc_ref, *s_ref):
    s_ref = s_ref[0] if defer else None

    def score(hd):
        lo = A_PAD * hd
        return jnp.dot(q_ref[:, lo:lo + A_PAD], kt_ref[lo:lo + A_PAD, :], preferred_element_type=F32)

    def finish():
        head = _head_of_lane(o_ref.shape)
        out = jnp.zeros(o_ref.shape, F32)
        for hd in range(N_HEADS):
            out = out + jnp.where(head == hd, acc_ref[hd] * (1.0 / _row_total(l_ref[hd])), 0.0)
        o_ref[...] = out.astype(o_ref.dtype)

    _dense_maps(N_HEADS, defer, score, _value_block(v_ref, kt_ref.shape[1]), m_ref, l_ref, acc_ref, s_ref, finish)


def _dense_specs(bsz, t_len, tq, tk, q_cols, k_rows):
    nq, nk = t_len // tq, t_len // tk
    specs = [pl.BlockSpec((tq, q_cols), lambda b, i, j: (b * nq + i, 0)),
             pl.BlockSpec((k_rows, tk), lambda b, i, j: (0, b * nk + j)),
             pl.BlockSpec((t_len, MIX_COLS), lambda b, i, j: (b, 0), pipeline_mode=pl.Buffered(1))]
    out_spec = pl.BlockSpec((tq, MIX_COLS), lambda b, i, j: (b * nq + i, 0))
    return (bsz, nq, nk), specs, out_spec


def _dense_scratch(n_maps, tq, tk, defer):
    shapes = [pltpu.VMEM((n_maps, tq, 1), F32), pltpu.VMEM((n_maps, tq, LANES), F32),
              pltpu.VMEM((n_maps, tq, MIX_COLS), F32)]
    return shapes + ([pltpu.VMEM((tq, tk), F32)] if defer else [])


def _mla_call(q, kt, v, bsz, t_len, tq, tk):
    n = q.shape[0]
    defer = t_len // tk > 2
    grid, in_specs, out_spec = _dense_specs(bsz, t_len, tq, tk, N_HEADS * A_PAD, N_HEADS * A_PAD)
    return pl.pallas_call(
        functools.partial(_mla_kernel, defer), grid=grid, in_specs=in_specs, out_specs=out_spec,
        out_shape=jax.ShapeDtypeStruct((n, MIX_COLS), BF16),
        scratch_shapes=_dense_scratch(N_HEADS, tq, tk, defer),
        compiler_params=_cparams(("arbitrary", "arbitrary", "arbitrary")), name="attn_mla",
    )(q, kt, v)


def _diff_kernel(lam_init, defer, q_ref, kt_ref, v_ref, lam_ref, g_ref, o_ref, qm_ref, m_ref, l_ref, acc_ref, *s_ref):
    s_ref = s_ref[0] if defer else None
    n_maps = 2 * N_HEADS

    @pl.when(pl.program_id(2) == 0)
    def _():
        q = q_ref[...]
        group = lax.broadcasted_iota(jnp.int32, q.shape, 1) // C_DIM
        for mi in range(n_maps):
            qm_ref[mi] = jnp.where(group == mi, q, jnp.zeros_like(q))

    def score(mi):
        return jnp.dot(qm_ref[mi], kt_ref[...], preferred_element_type=F32)

    def finish():
        lam = lam_ref[...]
        head = _head_of_lane(o_ref.shape)
        o = jnp.zeros(o_ref.shape, F32)
        for hd in range(N_HEADS):
            oh = (acc_ref[2 * hd] * (1.0 / _row_total(l_ref[2 * hd]))
                  - lam * (acc_ref[2 * hd + 1] * (1.0 / _row_total(l_ref[2 * hd + 1]))))
            o = o + jnp.where(head == hd, oh, 0.0)
        o2 = o * o
        inv = jnp.zeros(o_ref.shape, F32)
        for hd in range(N_HEADS):
            ms = jnp.sum(jnp.where(head == hd, o2, 0.0), axis=1, keepdims=True) * (1.0 / HEAD_DIM)
            inv = inv + jnp.where(head == hd, lax.rsqrt(ms + RMS_EPS), 0.0)
        o_ref[...] = ((o * inv * g_ref[...]) * (1.0 - lam_init)).astype(o_ref.dtype)

    _dense_maps(n_maps, defer, score, _value_block(v_ref, kt_ref.shape[1]), m_ref, l_ref, acc_ref, s_ref, finish)


def _diff_call(q, kt, v, lam, g_tiled, lam_init, bsz, t_len, tq, tk):
    n = q.shape[0]
    n_maps = 2 * N_HEADS
    defer = t_len // tk > 2
    grid, in_specs, out_spec = _dense_specs(bsz, t_len, tq, tk, MIX_COLS, MIX_COLS)
    in_specs += [pl.BlockSpec((1, 1), lambda b, i, j: (0, 0)), pl.BlockSpec((1, MIX_COLS), lambda b, i, j: (0, 0))]
    return pl.pallas_call(
        functools.partial(_diff_kernel, lam_init, defer), grid=grid, in_specs=in_specs, out_specs=out_spec,
        out_shape=jax.ShapeDtypeStruct((n, MIX_COLS), BF16),
        scratch_shapes=[pltpu.VMEM((n_maps, tq, MIX_COLS), BF16)] + _dense_scratch(n_maps, tq, tk, defer),
        compiler_params=_cparams(("arbitrary", "arbitrary", "arbitrary")), name="attn_diff",
    )(q, kt, v, lam, g_tiled)


def _window_heads(q, kw, vw, add, mult, o_ref):
    head = _head_of_lane(q.shape)
    out = jnp.zeros(q.shape, F32)
    for hd in range(N_HEADS):
        qh = jnp.where(head == hd, q, jnp.zeros_like(q))
        s = _nt_dot(qh, kw)
        if add is not None:
            s = s + add(hd)
        if mult is not None:
            s = jnp.where(mult > 0.0, s, NEG_BIG)
        m = jnp.max(s, axis=1, keepdims=True)
        p = jnp.exp2(s - m)
        if mult is not None:
            p = p * mult
        l = jnp.sum(p, axis=1, keepdims=True)
        o = jnp.dot(p.astype(BF16), vw, preferred_element_type=F32)
        out = out + jnp.where(head == hd, o * (1.0 / l), 0.0)
    o_ref[...] = out.astype(o_ref.dtype)


def _nbr_kernel(nb, q_ref, k_ref, v_ref, bias_ref, o_ref):
    i = pl.program_id(1)
    start = pl.multiple_of(jnp.clip(i - 1, 0, nb - B_WIN // QBLK) * QBLK, QBLK)
    kw = k_ref[pl.ds(start, B_WIN), :]
    vw = v_ref[pl.ds(start, B_WIN), :]
    _window_heads(q_ref[...], kw, vw, lambda hd: bias_ref[0, hd], None, o_ref)


def _nbr_block_type(i, nb):
    return jnp.where(i < 1, 0, jnp.where(i > nb - 2, 2, 1))


def _nbr_call(q, k, v, bias, bsz, t_len):
    n = q.shape[0]
    nb = t_len // QBLK
    return pl.pallas_call(
        functools.partial(_nbr_kernel, nb), grid=(bsz, nb),
        in_specs=[pl.BlockSpec((QBLK, MIX_COLS), lambda b, i: (b * nb + i, 0)),
                  pl.BlockSpec((t_len, MIX_COLS), lambda b, i: (b, 0)),
                  pl.BlockSpec((t_len, MIX_COLS), lambda b, i: (b, 0)),
                  pl.BlockSpec((1, N_HEADS, QBLK, B_WIN), lambda b, i: (_nbr_block_type(i, nb), 0, 0, 0))],
        out_specs=pl.BlockSpec((QBLK, MIX_COLS), lambda b, i: (b * nb + i, 0)),
        out_shape=jax.ShapeDtypeStruct((n, MIX_COLS), BF16),
        compiler_params=_cparams(("parallel", "arbitrary")), name="attn_nbr",
    )(q, k, v, bias)


def _nbr_bias_tables(rpb, t_len):
    rows = t_len // GRID_W
    nb = t_len // QBLK
    kr = min(NA_ROWS, rows)
    reps = (0, 1, nb - 1)
    n_dr, n_dc = 2 * NA_ROWS - 1, 2 * NA_COLS - 1
    q_rows, w_rows = QBLK // GRID_W, B_WIN // GRID_W
    qc = np.arange(GRID_W)[:, None]
    kc = np.arange(GRID_W)[None, :]
    cs = np.clip(qc - NA_COLS // 2, 0, GRID_W - NA_COLS)
    col_ok = (kc >= cs) & (kc < cs + NA_COLS)
    sel_c = ((kc - qc + NA_COLS - 1)[None] == np.arange(n_dc)[:, None, None]) & col_ok[None]
    sel_r = np.zeros((len(reps), q_rows, w_rows, n_dr), np.float32)
    row_ok = np.zeros((len(reps), q_rows, w_rows), bool)
    for ti, b in enumerate(reps):
        start_row = int(np.clip(b - 1, 0, nb - B_WIN // QBLK)) * q_rows
        for qr in range(q_rows):
            r = b * q_rows + qr
            rs = int(np.clip(r - kr // 2, 0, rows - kr))
            for wr in range(w_rows):
                key_r = start_row + wr
                if rs <= key_r < rs + kr:
                    row_ok[ti, qr, wr] = True
                    sel_r[ti, qr, wr, key_r - r + NA_ROWS - 1] = 1.0
    hp = lax.Precision.HIGHEST
    by_col = jnp.einsum("hrd,dqk->hrqk", rpb.astype(F32), jnp.asarray(sel_c, F32), precision=hp)
    dense = jnp.einsum("tawr,hrqk->thaqwk", jnp.asarray(sel_r), by_col, precision=hp) * LOG2E
    ok = row_ok[:, None, :, None, :, None] & col_ok[None, None, None, :, None, :]
    return jnp.where(ok, dense, NEG_BIG).reshape(len(reps), N_HEADS, QBLK, B_WIN)


def _dil_kernel(t_len, tq, q_ref, k_ref, v_ref, mask_ref, o_ref):
    i = pl.program_id(1)
    w = tq + 2 * D_REACH
    start = pl.multiple_of(jnp.clip(i * tq - D_REACH, 0, t_len - w), tq)
    kw = k_ref[pl.ds(start, w), :]
    vw = v_ref[pl.ds(start, w), :]
    _window_heads(q_ref[...], kw, vw, lambda hd: mask_ref[0], None, o_ref)


def _dil_masks(tq):
    w = tq + 2 * D_REACH
    n_place = 2 * D_REACH // tq + 1
    place = jnp.arange(n_place, dtype=jnp.int32)[:, None, None] * tq
    d = (lax.broadcasted_iota(jnp.int32, (n_place, tq, w), 2) - place
         - lax.broadcasted_iota(jnp.int32, (n_place, tq, w), 1))
    ad = jnp.abs(d)
    count = ((ad <= 64).astype(F32)
             + (((d & 3) == 0) & (ad <= 256)).astype(F32)
             + (((d & 15) == 0) & (ad <= D_REACH)).astype(F32))
    return jnp.where(count > 0.0, jnp.log2(jnp.maximum(count, 1.0)), NEG_BIG)


def _dil_call(q, k, v, masks, bsz, t_len, tq):
    n = q.shape[0]
    nq = t_len // tq
    w = tq + 2 * D_REACH
    n_place = masks.shape[0]
    half = D_REACH // tq

    def placement(i):
        return jnp.where(i < half, i, jnp.where(i > nq - 1 - half, i - nq + n_place, half))

    whole = lambda: pl.BlockSpec((t_len, MIX_COLS), lambda b, i: (b, 0), pipeline_mode=pl.Buffered(1))
    return pl.pallas_call(
        functools.partial(_dil_kernel, t_len, tq), grid=(bsz, nq),
        in_specs=[pl.BlockSpec((tq, MIX_COLS), lambda b, i: (b * nq + i, 0)), whole(), whole(),
                  pl.BlockSpec((1, tq, w), lambda b, i: (placement(i), 0, 0))],
        out_specs=pl.BlockSpec((tq, MIX_COLS), lambda b, i: (b * nq + i, 0)),
        out_shape=jax.ShapeDtypeStruct((n, MIX_COLS), BF16),
        compiler_params=_cparams(("arbitrary", "arbitrary")), name="attn_dil",
    )(q, k, v, masks)


def _out_kernel(oa_ref, ob_ref, oc_ref, od_ref, x_ref, wo_ref, g2_ref, wr_ref, x2_ref, hn_ref, aff_ref):
    acc = x_ref[...]
    for mi, o_ref in enumerate((oa_ref, ob_ref, oc_ref, od_ref)):
        acc = acc + jnp.dot(o_ref[...], wo_ref[MIX_COLS * mi:MIX_COLS * (mi + 1), :], preferred_element_type=F32)
    x2_ref[...] = acc
    hn = _rms(acc) * g2_ref[...]
    hn_hi = hn.astype(BF16)
    hn_ref[...] = hn_hi
    hn_lo = (hn - hn_hi.astype(F32)).astype(BF16)
    logits = (jnp.dot(hn_hi, wr_ref[0], preferred_element_type=F32)
              + jnp.dot(hn_lo, wr_ref[0], preferred_element_type=F32)
              + jnp.dot(hn_hi, wr_ref[1], preferred_element_type=F32))
    lane = lax.broadcasted_iota(jnp.int32, logits.shape, 1)
    logits = jnp.where(lane < N_EXPERTS, logits, -jnp.inf)
    m = jnp.max(logits, axis=1, keepdims=True)
    e = jnp.exp(logits - m)
    aff_ref[...] = e / jnp.sum(e, axis=1, keepdims=True)


def _out_call(oa, ob, oc, od, x2d, wo, g2, wr, tm):
    n = x2d.shape[0]
    full = lambda a: pl.BlockSpec(a.shape, lambda i: (0,) * a.ndim)
    row = lambda w: pl.BlockSpec((tm, w), lambda i: (i, 0))
    return pl.pallas_call(
        _out_kernel, grid=(n // tm,),
        in_specs=[row(MIX_COLS)] * 4 + [row(D_MODEL), full(wo), full(g2), full(wr)],
        out_specs=(row(D_MODEL), row(D_MODEL), row(LANES)),
        out_shape=(jax.ShapeDtypeStruct((n, D_MODEL), F32), jax.ShapeDtypeStruct((n, D_MODEL), BF16),
                   jax.ShapeDtypeStruct((n, LANES), F32)),
        compiler_params=_cparams(("parallel",)), name="proj_out",
    )(oa, ob, oc, od, x2d, wo, g2, wr)


FLAG_FIRST, FLAG_LAST, FLAG_SUB = 1, 2, 4
Y_BLK = 128
GATHER_FANIN = 4
COMBINE_FANIN = 8


def _count_le(sorted_vals, queries):
    return jnp.sum((sorted_vals[None, :] <= queries[:, None]).astype(jnp.int32), axis=1)


def _step_lists(cnt, fanin, n_steps):
    steps = jnp.maximum((cnt + fanin - 1) // fanin, 1)
    ends = jnp.cumsum(steps)
    w = jnp.arange(n_steps, dtype=jnp.int32)
    grp = jnp.minimum(_count_le(ends, w), cnt.shape[0] - 1)
    per_grp = jnp.stack([ends - steps, steps, cnt], axis=1)[grp]
    rank = w - per_grp[:, 0]
    valid = w < ends[-1]
    item0 = rank * fanin
    flags = (jnp.where(valid & (rank == 0), FLAG_FIRST, 0)
             | jnp.where(valid & (rank == per_grp[:, 1] - 1), FLAG_LAST, 0))
    return grp, item0, flags, jnp.where(valid, per_grp[:, 2] - item0, 0)


def _with_live_flags(flags, remaining, slot_ranks):
    for k, r in enumerate(slot_ranks):
        flags = flags | jnp.where(r < remaining, FLAG_SUB << k, 0)
    return flags.astype(jnp.int32)


def _route_tables(aff, n, cap, ts, tc, tt):
    gate, idx = lax.top_k(aff[:, :N_EXPERTS].T, cap)
    idx, gate = lax.sort((idx, gate), dimension=1, num_keys=1)
    idx = idx.astype(jnp.int32)

    tiles_per_e = cap // ts
    n_tiles = N_EXPERTS * tiles_per_e
    n_chunks = n // tc
    c0 = (idx[:, ::ts] // tc).reshape(n_tiles)
    c1 = (idx[:, ts - 1::ts] // tc).reshape(n_tiles)
    n_gs = (N_EXPERTS * n_chunks + n_tiles) // GATHER_FANIN + n_tiles
    g_tile, g_item0, g_flags, g_left = _step_lists(c1 - c0 + 1, GATHER_FANIN, n_gs)
    g_first = c0[g_tile] + g_item0
    g_ranks = [(k - g_first) % GATHER_FANIN for k in range(GATHER_FANIN)]
    g_flags = _with_live_flags(g_flags, g_left, g_ranks)
    g_chunks = jnp.concatenate([jnp.minimum(g_first + r, n_chunks - 1) for r in g_ranks])

    n_tt = n // tt
    blk_per_e = cap // Y_BLK
    bounds = jnp.arange(n_tt + 1, dtype=jnp.int32) * tt
    pos = jnp.sum((idx[:, None, :] < bounds[None, :, None]).astype(jnp.int32), axis=2)
    lo, hi = pos[:, :-1], pos[:, 1:]
    b0 = jnp.minimum(lo // Y_BLK, blk_per_e - 1)
    b1 = jnp.maximum((hi - 1) // Y_BLK, b0)
    pair_cnt = jnp.where(hi > lo, b1 - b0 + 1, 0).T.reshape(-1)
    pair_b0 = (b0 + (jnp.arange(N_EXPERTS, dtype=jnp.int32) * blk_per_e)[:, None]).T.reshape(-1)
    n_items = N_EXPERTS * blk_per_e + N_EXPERTS * n_tt
    item_ends = jnp.cumsum(pair_cnt)
    it = jnp.arange(n_items, dtype=jnp.int32)
    pair = jnp.minimum(_count_le(item_ends, it), pair_cnt.shape[0] - 1)
    item_blk = jnp.minimum(pair_b0[pair] + it - (item_ends - pair_cnt)[pair], N_EXPERTS * blk_per_e - 1)
    tile_cnt = jnp.sum(pair_cnt.reshape(n_tt, N_EXPERTS), axis=1)
    tile_item0 = jnp.cumsum(tile_cnt) - tile_cnt
    n_cs = n_items // COMBINE_FANIN + n_tt
    c_tile, c_item0, c_flags, c_left = _step_lists(tile_cnt, COMBINE_FANIN, n_cs)
    c_flags = _with_live_flags(c_flags, c_left, list(range(COMBINE_FANIN)))
    c_first = tile_item0[c_tile] + c_item0
    c_blks = jnp.concatenate([item_blk[jnp.minimum(c_first + k, n_items - 1)] for k in range(COMBINE_FANIN)])
    return idx, gate, (g_tile, g_chunks, g_flags), (c_tile, c_blks, c_flags)


def _ffn_kernel(tc, tile_ref, chunk_ref, flag_ref, tok_ref, gate_ref, *refs):
    hn_refs = refs[:GATHER_FANIN]
    wg_ref, wu_ref, wd_ref, y_ref, xacc = refs[GATHER_FANIN:]
    w = pl.program_id(0)
    n_steps = pl.num_programs(0)
    flags = flag_ref[w]

    @pl.when((flags & FLAG_FIRST) != 0)
    def _():
        xacc[...] = jnp.zeros(xacc.shape, F32)

    for k in range(GATHER_FANIN):
        @pl.when((flags & (FLAG_SUB << k)) != 0)
        def _(k=k):
            ts = tok_ref.shape[0]
            token = chunk_ref[k * n_steps + w] * tc + lax.broadcasted_iota(jnp.int32, (ts, tc), 1)
            onehot = jnp.where(tok_ref[...] == token, 1.0, 0.0).astype(BF16)
            xacc[...] += jnp.dot(onehot, hn_refs[k][...], preferred_element_type=F32)

    @pl.when((flags & FLAG_LAST) != 0)
    def _():
        xg = xacc[...].astype(BF16)
        a = jnp.dot(xg, wg_ref[0, 0], preferred_element_type=F32)
        b = jnp.dot(xg, wu_ref[0, 0], preferred_element_type=F32)
        hid = (a * jax.nn.sigmoid(a) * b).astype(BF16)
        y_ref[...] = (jnp.dot(hid, wd_ref[0, 0], preferred_element_type=F32) * gate_ref[...]).astype(BF16)


def _ffn_call(steps, idx, gate, hn, w_gate, w_up, w_down, layer, ts, tc):
    n_exp, cap = idx.shape
    tiles_per_e = cap // ts
    tile, chunks, flags = steps
    n_steps = tile.shape[0]
    tok = idx.reshape(n_exp * cap, 1)
    gate2 = gate.reshape(n_exp * cap, 1)
    slot = lambda wd: pl.BlockSpec((ts, wd), lambda w, tile, chunks, flags: (tile[w], 0))
    chunk = lambda k: pl.BlockSpec((tc, D_MODEL), lambda w, tile, chunks, flags: (chunks[k * n_steps + w], 0))
    wspec = pl.BlockSpec((1, 1, D_MODEL, D_MODEL), lambda w, tile, chunks, flags: (layer, tile[w] // tiles_per_e, 0, 0))
    grid_spec = pltpu.PrefetchScalarGridSpec(
        num_scalar_prefetch=3, grid=(n_steps,),
        in_specs=[slot(1), slot(1)] + [chunk(k) for k in range(GATHER_FANIN)] + [wspec, wspec, wspec],
        out_specs=slot(D_MODEL),
        scratch_shapes=[pltpu.VMEM((ts, D_MODEL), F32)])
    return pl.pallas_call(
        functools.partial(_ffn_kernel, tc), grid_spec=grid_spec,
        out_shape=jax.ShapeDtypeStruct((n_exp * cap, D_MODEL), BF16),
        compiler_params=_cparams(("arbitrary",)), name="expert_ffn",
    )(tile, chunks, flags, tok, gate2, *([hn] * GATHER_FANIN), w_gate, w_up, w_down)


def _combine_kernel(tt, final, tile_ref, blk_ref, flag_ref, *refs):
    tok_refs = refs[:COMBINE_FANIN]
    y_refs = refs[COMBINE_FANIN:2 * COMBINE_FANIN]
    x_ref, g_ref, o_ref = refs[2 * COMBINE_FANIN:]
    w = pl.program_id(0)
    flags = flag_ref[w]

    @pl.when((flags & FLAG_FIRST) != 0)
    def _():
        o_ref[...] = x_ref[...]

    @pl.when((flags & FLAG_SUB) != 0)
    def _():
        token = tile_ref[w] * tt + lax.broadcasted_iota(jnp.int32, (tt, Y_BLK), 0)
        hots = []
        for k in range(COMBINE_FANIN):
            live = (flags & (FLAG_SUB << k)) != 0
            tok = jnp.where(live, tok_refs[k][0], -1)
            hots.append(jnp.where(token == tok, 1.0, 0.0).astype(BF16))
        onehot = jnp.concatenate(hots, axis=1)
        ycat = jnp.concatenate([y_refs[k][...] for k in range(COMBINE_FANIN)], axis=0)
        o_ref[...] += jnp.dot(onehot, ycat, preferred_element_type=F32)

    if final:
        @pl.when((flags & FLAG_LAST) != 0)
        def _():
            o_ref[...] = _rms(o_ref[...]) * g_ref[...]


def _combine_call(steps, idx, y, x2, g_final, tt, final):
    n = x2.shape[0]
    tile, blks, flags = steps
    n_steps = tile.shape[0]
    tok = idx.reshape(-1, 1, Y_BLK)
    tspec = lambda k: pl.BlockSpec((1, 1, Y_BLK), lambda w, tile, blks, flags: (blks[k * n_steps + w], 0, 0))
    yspec = lambda k: pl.BlockSpec((Y_BLK, D_MODEL), lambda w, tile, blks, flags: (blks[k * n_steps + w], 0))
    xspec = pl.BlockSpec((tt, D_MODEL), lambda w, tile, blks, flags: (tile[w], 0))
    fan = range(COMBINE_FANIN)
    grid_spec = pltpu.PrefetchScalarGridSpec(
        num_scalar_prefetch=3, grid=(n_steps,),
        in_specs=[tspec(k) for k in fan] + [yspec(k) for k in fan]
        + [xspec, pl.BlockSpec((1, D_MODEL), lambda w, tile, blks, flags: (0, 0))],
        out_specs=xspec)
    return pl.pallas_call(
        functools.partial(_combine_kernel, tt, final), grid_spec=grid_spec,
        out_shape=jax.ShapeDtypeStruct((n, D_MODEL), F32),
        compiler_params=_cparams(("arbitrary",)), name="expert_combine",
    )(tile, blks, flags, *([tok] * COMBINE_FANIN), *([y] * COMBINE_FANIN), x2, g_final)


def _rot_cols(w, d):
    k, c = w.shape
    half = d // 2
    wg = w.reshape(k, c // d, 2, half)
    return jnp.concatenate([-wg[:, :, 1], wg[:, :, 0]], axis=2).reshape(k, c)


def _rope_tables(t_len):
    pos = jnp.arange(t_len, dtype=F32)

    def cs(d):
        half = d // 2
        inv = ROPE_THETA ** (-jnp.arange(half, dtype=F32) / half)
        ang = pos[:, None] * inv[None, :]
        return (jnp.concatenate([jnp.cos(ang)] * 2, axis=1), jnp.concatenate([jnp.sin(ang)] * 2, axis=1))

    c32, s32 = cs(A_ROPE)
    c64, s64 = cs(HEAD_DIM)
    ones = jnp.ones((t_len, A_NOPE), F32)
    zeros = jnp.zeros((t_len, A_NOPE), F32)
    pad = jnp.zeros((t_len, A_PAD - A_NOPE - A_ROPE), F32)
    tab = jnp.concatenate([ones, c32, pad, zeros, s32, pad,
                           jnp.tile(c32, (1, LANES // A_ROPE)), jnp.tile(s32, (1, LANES // A_ROPE)),
                           jnp.tile(c64, (1, LANES // HEAD_DIM)), jnp.tile(s64, (1, LANES // HEAD_DIM))], axis=1)
    tabt = jnp.concatenate([c32.T, s32.T], axis=0)
    return tab, tabt


def _layer_weights(l, w_in, a_w_uq, a_w_ukv, w_out, w_router):
    pts = np.cumsum(PROJ_SIZES)[:-1]
    (a_cq, a_ckv, a_kr, b_q, b_k, b_v, c_q, c_k, c_v, d_q, d_k, d_v) = jnp.split(w_in[l], pts, axis=1)
    wm = jnp.concatenate([a_cq, a_ckv, b_q, b_k, b_v, c_q, _rot_cols(c_q, C_DIM), c_v,
                          d_q, _rot_cols(d_q, HEAD_DIM), d_k, _rot_cols(d_k, HEAD_DIM), d_v], axis=1).astype(BF16)
    wt = jnp.concatenate([c_k, _rot_cols(c_k, C_DIM), a_kr, _rot_cols(a_kr, A_ROPE)], axis=1).T.astype(BF16)
    uq = a_w_uq[l].reshape(A_Q_LORA, N_HEADS, A_NOPE + A_ROPE)
    zpad = jnp.zeros((A_Q_LORA, N_HEADS, A_PAD - A_NOPE - A_ROPE), F32)
    uq_rope = uq[:, :, A_NOPE:]
    uq_rot = _rot_cols(uq_rope.reshape(A_Q_LORA, N_HEADS * A_ROPE), A_ROPE).reshape(A_Q_LORA, N_HEADS, A_ROPE)
    plain = jnp.concatenate([uq, zpad], axis=2).reshape(A_Q_LORA, N_HEADS * A_PAD)
    rot = jnp.concatenate([jnp.zeros_like(uq[:, :, :A_NOPE]), uq_rot, zpad], axis=2).reshape(A_Q_LORA, N_HEADS * A_PAD)
    wuq = jnp.concatenate([plain, rot], axis=1).astype(BF16)
    ukv = a_w_ukv[l].reshape(A_KV_LORA, N_HEADS, 2 * HEAD_DIM)
    wukt = ukv[:, :, :A_NOPE].reshape(A_KV_LORA, MIX_COLS).T.astype(BF16)
    wuv = ukv[:, :, A_NOPE:].reshape(A_KV_LORA, MIX_COLS).astype(BF16)
    wo = w_out[l].astype(BF16)
    wr = jnp.concatenate([w_router[l], jnp.zeros((D_MODEL, LANES - N_EXPERTS), F32)], axis=1)
    wr_hi = wr.astype(BF16)
    wr = jnp.stack([wr_hi, (wr - wr_hi.astype(F32)).astype(BF16)])
    return wm, wt, wuq, wukt, wuv, wo, wr


def _trunk(x, norm1_g, w_in, a_qnorm_g, a_w_uq, a_kvnorm_g, a_w_ukv, b_rpb, c_lambda, c_subln_g, w_out,
           norm2_g, w_router, w_gate, w_up, w_down, final_g):
    bsz, t_len, _ = x.shape
    n = bsz * t_len
    tm = 512
    tq = 512
    tk_mla, tk_diff = min(4096, t_len), 2048
    tq_dil = 256
    ts, tc, tt = 256, 1024, 512
    depth = w_in.shape[0]
    cap = CAP_FACTOR * n // N_EXPERTS
    assert t_len % tk_mla == 0 and t_len % tk_diff == 0 and t_len >= tq_dil + 2 * D_REACH and t_len // QBLK >= 3
    assert cap % ts == 0 and n % tc == 0 and n % tt == 0
    tab, tabt = _rope_tables(t_len)
    dil_masks = _dil_masks(tq_dil)
    x2d = x.reshape(n, D_MODEL)
    g_final = final_g.reshape(1, D_MODEL)
    for l in range(depth):
        lam_init = 0.8 - 0.6 * math.exp(-0.3 * l)
        wm, wt, wuq, wukt, wuv, wo, wr = _layer_weights(l, w_in, a_w_uq, a_w_ukv, w_out, w_router)
        row = lambda g: g[l].reshape(1, -1)
        (qa, kat, va, qb, kb, vb, qc, kct, vc, qd, kd, vd) = _proj_call(
            x2d, t_len, row(norm1_g), wm, wt, row(a_qnorm_g), wuq, row(a_kvnorm_g), wukt, wuv, tab, tabt, tm)
        oa = _mla_call(qa, kat, va, bsz, t_len, tq, tk_mla)
        ob = _nbr_call(qb, kb, vb, _nbr_bias_tables(b_rpb[l], t_len), bsz, t_len)
        lp = c_lambda[l].astype(F32)
        lam = (jnp.exp(jnp.sum(lp[0] * lp[1])) - jnp.exp(jnp.sum(lp[2] * lp[3])) + lam_init).reshape(1, 1)
        g_sub = jnp.tile(c_subln_g[l], N_HEADS).reshape(1, MIX_COLS)
        oc = _diff_call(qc, kct, vc, lam, g_sub, lam_init, bsz, t_len, tq, tk_diff)
        od = _dil_call(qd, kd, vd, dil_masks, bsz, t_len, tq_dil)
        x2, hn, aff = _out_call(oa, ob, oc, od, x2d, wo, row(norm2_g), wr, tm)
        idx, gate, g_items, c_items = _route_tables(aff, n, cap, ts, tc, tt)
        y = _ffn_call(g_items, idx, gate, hn, w_gate, w_up, w_down, l, ts, tc)
        x2d = _combine_call(c_items, idx, y, x2, g_final, tt, l == depth - 1)
    return x2d.reshape(bsz, t_len, D_MODEL)


def kernel(x_prompt, x_sample, norm1_g, w_in, a_qnorm_g, a_w_uq, a_kvnorm_g, a_w_ukv, b_rpb, c_lambda, c_subln_g,
           w_out, norm2_g, w_router, w_gate, w_up, w_down, final_g):
    params = (norm1_g, w_in, a_qnorm_g, a_w_uq, a_kvnorm_g, a_w_ukv, b_rpb, c_lambda, c_subln_g, w_out,
              norm2_g, w_router, w_gate.astype(BF16), w_up.astype(BF16), w_down.astype(BF16), final_g)
    return (_trunk(x_prompt, *params), _trunk(x_sample, *params))
```

```python
import functools
import math

import numpy as np
import jax
import jax.numpy as jnp
from jax import lax
from jax.experimental import pallas as pl
from jax.experimental.pallas import tpu as pltpu

F32 = jnp.float32
BF16 = jnp.bfloat16

D_MODEL = 1024
N_HEADS = 4
HEAD_DIM = 64
MIX_COLS = N_HEADS * HEAD_DIM
A_NOPE = 64
A_ROPE = 32
A_PAD = 128
A_Q_LORA = 256
A_KV_LORA = 128
C_DIM = 32
GRID_W = 64
NA_ROWS = 8
NA_COLS = 16
QBLK = 256
B_WIN = 3 * QBLK
D_REACH = 1024
N_EXPERTS = 16
CAP_FACTOR = 2
ROPE_THETA = 10000.0
RMS_EPS = 1e-6
NEG_BIG = -1e30
LOG2E = math.log2(math.e)
LANES = 128
V7X_VMEM_LIMIT = 56 * 1024 * 1024

PROJ_SIZES = (256, 128, 32, 256, 256, 256, 256, 256, 256, 256, 256, 256)

_M_ACQ, _M_ACKV, _M_BQ, _M_BK, _M_BV = 0, 256, 384, 640, 896
_M_CQ, _M_CQR, _M_CV = 1152, 1408, 1664
_M_DQ, _M_DQR, _M_DK, _M_DKR, _M_DV = 1920, 2176, 2432, 2688, 2944
_M_COLS = 3200
_T_CK, _T_CKR, _T_AKR, _T_AKRR, _T_ROWS = 0, 256, 512, 544, 576


def _cparams(sem):
    return pltpu.CompilerParams(dimension_semantics=sem, vmem_limit_bytes=V7X_VMEM_LIMIT)


def _rms(x):
    return x * lax.rsqrt(jnp.mean(x * x, axis=-1, keepdims=True) + RMS_EPS)


def _nt_dot(a, b):
    return lax.dot_general(a, b, (((1,), (1,)), ((), ())), preferred_element_type=F32)


def _head_of_lane(shape):
    return lax.broadcasted_iota(jnp.int32, shape, len(shape) - 1) // HEAD_DIM


def _proj_kernel(x_ref, g1_ref, wm_ref, wt_ref, gq_ref, wuq_ref, gkv_ref, wukt_ref, wuv_ref,
                 tab_ref, tabt_ref,
                 qa_ref, kat_ref, va_ref, qb_ref, kb_ref, vb_ref, qc_ref, kct_ref, vc_ref,
                 qd_ref, kd_ref, vd_ref):
    x = x_ref[...]
    h = (_rms(x) * g1_ref[...]).astype(BF16)
    p = jnp.dot(h, wm_ref[...], preferred_element_type=F32)
    pt = _nt_dot(wt_ref[...], h)
    tab = tab_ref[...]
    cos_a, sin_a, cos_c, sin_c, cos_d, sin_d = (tab[:, LANES * i:LANES * (i + 1)] for i in range(6))
    tabt = tabt_ref[...]
    cos_t, sin_t = tabt[0:A_ROPE], tabt[A_ROPE:2 * A_ROPE]

    scale_a = LOG2E * (A_NOPE + A_ROPE) ** -0.5
    latq = (_rms(p[:, _M_ACQ:_M_ACQ + A_Q_LORA]) * gq_ref[...]).astype(BF16)
    qa2 = jnp.dot(latq, wuq_ref[...], preferred_element_type=F32)
    for hd in range(N_HEADS):
        lo = A_PAD * hd
        blk = qa2[:, lo:lo + A_PAD] * cos_a + qa2[:, N_HEADS * A_PAD + lo:N_HEADS * A_PAD + lo + A_PAD] * sin_a
        qa_ref[:, lo:lo + A_PAD] = (blk * scale_a).astype(BF16)
    latkv = (_rms(p[:, _M_ACKV:_M_ACKV + A_KV_LORA]) * gkv_ref[...]).astype(BF16)
    va_ref[...] = jnp.dot(latkv, wuv_ref[...], preferred_element_type=F32).astype(BF16)
    knt = _nt_dot(wukt_ref[...], latkv)
    krt = (pt[_T_AKR:_T_AKR + A_ROPE] * cos_t + pt[_T_AKRR:_T_AKRR + A_ROPE] * sin_t).astype(BF16)
    tm = x.shape[0]
    for hd in range(N_HEADS):
        lo = A_PAD * hd
        kat_ref[lo:lo + A_NOPE, :] = knt[A_NOPE * hd:A_NOPE * (hd + 1)].astype(BF16)
        kat_ref[lo + A_NOPE:lo + A_NOPE + A_ROPE, :] = krt
        kat_ref[lo + A_NOPE + A_ROPE:lo + A_PAD, :] = jnp.zeros((A_PAD - A_NOPE - A_ROPE, tm), BF16)

    qb_ref[...] = (p[:, _M_BQ:_M_BQ + MIX_COLS] * (LOG2E * HEAD_DIM ** -0.5)).astype(BF16)
    kb_ref[...] = p[:, _M_BK:_M_BK + MIX_COLS].astype(BF16)
    vb_ref[...] = p[:, _M_BV:_M_BV + MIX_COLS].astype(BF16)

    scale_c = LOG2E * C_DIM ** -0.5
    for j in range(MIX_COLS // LANES):
        lo = LANES * j
        blk = p[:, _M_CQ + lo:_M_CQ + lo + LANES] * cos_c + p[:, _M_CQR + lo:_M_CQR + lo + LANES] * sin_c
        qc_ref[:, lo:lo + LANES] = (blk * scale_c).astype(BF16)
    reps = MIX_COLS // A_ROPE
    cos_ct = jnp.concatenate([cos_t] * reps, axis=0)
    sin_ct = jnp.concatenate([sin_t] * reps, axis=0)
    kct_ref[...] = (pt[_T_CK:_T_CK + MIX_COLS] * cos_ct + pt[_T_CKR:_T_CKR + MIX_COLS] * sin_ct).astype(BF16)
    vc_ref[...] = p[:, _M_CV:_M_CV + MIX_COLS].astype(BF16)

    for j in range(MIX_COLS // LANES):
        lo = LANES * j
        qblk = p[:, _M_DQ + lo:_M_DQ + lo + LANES] * cos_d + p[:, _M_DQR + lo:_M_DQR + lo + LANES] * sin_d
        qd_ref[:, lo:lo + LANES] = (qblk * (LOG2E * HEAD_DIM ** -0.5)).astype(BF16)
        kblk = p[:, _M_DK + lo:_M_DK + lo + LANES] * cos_d + p[:, _M_DKR + lo:_M_DKR + lo + LANES] * sin_d
        kd_ref[:, lo:lo + LANES] = kblk.astype(BF16)
    vd_ref[...] = p[:, _M_DV:_M_DV + MIX_COLS].astype(BF16)


def _proj_call(x2d, t_len, g1, wm, wt, gq, wuq, gkv, wukt, wuv, tab, tabt, tm):
    n = x2d.shape[0]
    nt = t_len // tm
    full = lambda a: pl.BlockSpec(a.shape, lambda i: (0,) * a.ndim)
    row = lambda w: pl.BlockSpec((tm, w), lambda i: (i, 0))
    col = lambda r: pl.BlockSpec((r, tm), lambda i: (0, i))
    tok = lambda w: jax.ShapeDtypeStruct((n, w), BF16)
    out_shape = (tok(N_HEADS * A_PAD), jax.ShapeDtypeStruct((N_HEADS * A_PAD, n), BF16), tok(MIX_COLS),
                 tok(MIX_COLS), tok(MIX_COLS), tok(MIX_COLS),
                 tok(MIX_COLS), jax.ShapeDtypeStruct((MIX_COLS, n), BF16), tok(MIX_COLS),
                 tok(MIX_COLS), tok(MIX_COLS), tok(MIX_COLS))
    out_specs = (row(N_HEADS * A_PAD), col(N_HEADS * A_PAD), row(MIX_COLS),
                 row(MIX_COLS), row(MIX_COLS), row(MIX_COLS),
                 row(MIX_COLS), col(MIX_COLS), row(MIX_COLS),
                 row(MIX_COLS), row(MIX_COLS), row(MIX_COLS))
    in_specs = [row(D_MODEL), full(g1), full(wm), full(wt), full(gq), full(wuq), full(gkv), full(wukt), full(wuv),
                pl.BlockSpec((tm, tab.shape[1]), lambda i: (i % nt, 0)),
                pl.BlockSpec((tabt.shape[0], tm), lambda i: (0, i % nt))]
    return pl.pallas_call(
        _proj_kernel, grid=(n // tm,), in_specs=in_specs, out_specs=out_specs, out_shape=out_shape,
        compiler_params=_cparams(("parallel",)), name="proj_in",
    )(x2d, g1, wm, wt, gq, wuq, gkv, wukt, wuv, tab, tabt)


def _row_total(l_lanes):
    return jnp.sum(l_lanes, axis=1, keepdims=True)


def _online_step(s, v, m_ref, l_ref, acc_ref, idx, first, keep=None):
    m_prev = jnp.where(first, -jnp.inf, m_ref[idx])
    l_prev = jnp.where(first, 0.0, l_ref[idx])
    acc_prev = jnp.where(first, 0.0, acc_ref[idx])
    m_new = jnp.maximum(m_prev, jnp.max(s, axis=1, keepdims=True))
    alpha = jnp.exp2(m_prev - m_new)
    p = jnp.exp2(s - m_new)
    part = p[:, 0:LANES]
    for c in range(1, s.shape[1] // LANES):
        part = part + p[:, LANES * c:LANES * (c + 1)]
    l_new = alpha * l_prev + part
    acc_new = alpha * acc_prev + jnp.dot(p.astype(BF16), v, preferred_element_type=F32)
    if keep is not None:
        l_new, acc_new = jnp.where(keep, l_new, 0.0), jnp.where(keep, acc_new, 0.0)
        m_new = jnp.where(keep, m_new, -jnp.inf)
    l_ref[idx] = l_new
    acc_ref[idx] = acc_new
    m_ref[idx] = m_new


def _zero_state_once(refs):
    @pl.when((pl.program_id(0) == 0) & (pl.program_id(1) == 0) & (pl.program_id(2) == 0))
    def _():
        for r in refs:
            r[...] = jnp.zeros(r.shape, r.dtype)


def _dense_maps(n_maps, defer, score, v_block, m_ref, l_ref, acc_ref, s_ref, finish):
    j = pl.program_id(2)
    first = j == 0
    is_last = j == pl.num_programs(2) - 1
    _zero_state_once([m_ref, l_ref, acc_ref] + ([s_ref] if defer else []))
    if defer:
        keep = jnp.logical_not(first)
        _online_step(s_ref[...], v_block(jnp.maximum(j - 1, 0)), m_ref, l_ref, acc_ref, n_maps - 1, first, keep)
    v = v_block(j)
    for mi in range(n_maps - 1 if defer else n_maps):
        _online_step(score(mi), v, m_ref, l_ref, acc_ref, mi, first)
    if defer:
        s_ref[...] = score(n_maps - 1)

    @pl.when(is_last)
    def _():
        if defer:
            _online_step(s_ref[...], v_block(j), m_ref, l_ref, acc_ref, n_maps - 1, False)
        finish()


def _value_block(v_ref, tk):
    return lambda j: v_ref[pl.ds(pl.multiple_of(j * tk, tk), tk), :]


def _mla_kernel(defer, q_ref, kt_ref, v_ref, o_ref, m_ref, l_ref, acc_ref, *s_ref):
    s_ref = s_ref[0] if defer else None

    def score(hd):
        lo = A_PAD * hd
        return jnp.dot(q_ref[:, lo:lo + A_PAD], kt_ref[lo:lo + A_PAD, :], preferred_element_type=F32)

    def finish():
        head = _head_of_lane(o_ref.shape)
        out = jnp.zeros(o_ref.shape, F32)
        for hd in range(N_HEADS):
            out = out + jnp.where(head == hd, acc_ref[hd] * (1.0 / _row_total(l_ref[hd])), 0.0)
        o_ref[...] = out.astype(o_ref.dtype)

    _dense_maps(N_HEADS, defer, score, _value_block(v_ref, kt_ref.shape[1]), m_ref, l_ref, acc_ref, s_ref, finish)


def _dense_specs(bsz, t_len, tq, tk, q_cols, k_rows):
    nq, nk = t_len // tq, t_len // tk
    specs = [pl.BlockSpec((tq, q_cols), lambda b, i, j: (b * nq + i, 0)),
             pl.BlockSpec((k_rows, tk), lambda b, i, j: (0, b * nk + j)),
             pl.BlockSpec((t_len, MIX_COLS), lambda b, i, j: (b, 0), pipeline_mode=pl.Buffered(1))]
    out_spec = pl.BlockSpec((tq, MIX_COLS), lambda b, i, j: (b * nq + i, 0))
    return (bsz, nq, nk), specs, out_spec


def _dense_scratch(n_maps, tq, tk, defer):
    shapes = [pltpu.VMEM((n_maps, tq, 1), F32), pltpu.VMEM((n_maps, tq, LANES), F32),
              pltpu.VMEM((n_maps, tq, MIX_COLS), F32)]
    return shapes + ([pltpu.VMEM((tq, tk), F32)] if defer else [])


def _mla_call(q, kt, v, bsz, t_len, tq, tk):
    n = q.shape[0]
    defer = t_len // tk > 2
    grid, in_specs, out_spec = _dense_specs(bsz, t_len, tq, tk, N_HEADS * A_PAD, N_HEADS * A_PAD)
    return pl.pallas_call(
        functools.partial(_mla_kernel, defer), grid=grid, in_specs=in_specs, out_specs=out_spec,
        out_shape=jax.ShapeDtypeStruct((n, MIX_COLS), BF16),
        scratch_shapes=_dense_scratch(N_HEADS, tq, tk, defer),
        compiler_params=_cparams(("arbitrary", "arbitrary", "arbitrary")), name="attn_mla",
    )(q, kt, v)


def _diff_kernel(lam_init, defer, q_ref, kt_ref, v_ref, lam_ref, g_ref, o_ref, qm_ref, m_ref, l_ref, acc_ref, *s_ref):
    s_ref = s_ref[0] if defer else None
    n_maps = 2 * N_HEADS

    @pl.when(pl.program_id(2) == 0)
    def _():
        q = q_ref[...]
        group = lax.broadcasted_iota(jnp.int32, q.shape, 1) // C_DIM
        for mi in range(n_maps):
            qm_ref[mi] = jnp.where(group == mi, q, jnp.zeros_like(q))

    def score(mi):
        return jnp.dot(qm_ref[mi], kt_ref[...], preferred_element_type=F32)

    def finish():
        lam = lam_ref[...]
        head = _head_of_lane(o_ref.shape)
        o = jnp.zeros(o_ref.shape, F32)
        for hd in range(N_HEADS):
            oh = (acc_ref[2 * hd] * (1.0 / _row_total(l_ref[2 * hd]))
                  - lam * (acc_ref[2 * hd + 1] * (1.0 / _row_total(l_ref[2 * hd + 1]))))
            o = o + jnp.where(head == hd, oh, 0.0)
        o2 = o * o
        inv = jnp.zeros(o_ref.shape, F32)
        for hd in range(N_HEADS):
            ms = jnp.sum(jnp.where(head == hd, o2, 0.0), axis=1, keepdims=True) * (1.0 / HEAD_DIM)
            inv = inv + jnp.where(head == hd, lax.rsqrt(ms + RMS_EPS), 0.0)
        o_ref[...] = ((o * inv * g_ref[...]) * (1.0 - lam_init)).astype(o_ref.dtype)

    _dense_maps(n_maps, defer, score, _value_block(v_ref, kt_ref.shape[1]), m_ref, l_ref, acc_ref, s_ref, finish)


def _diff_call(q, kt, v, lam, g_tiled, lam_init, bsz, t_len, tq, tk):
    n = q.shape[0]
    n_maps = 2 * N_HEADS
    defer = t_len // tk > 2
    grid, in_specs, out_spec = _dense_specs(bsz, t_len, tq, tk, MIX_COLS, MIX_COLS)
    in_specs += [pl.BlockSpec((1, 1), lambda b, i, j: (0, 0)), pl.BlockSpec((1, MIX_COLS), lambda b, i, j: (0, 0))]
    return pl.pallas_call(
        functools.partial(_diff_kernel, lam_init, defer), grid=grid, in_specs=in_specs, out_specs=out_spec,
        out_shape=jax.ShapeDtypeStruct((n, MIX_COLS), BF16),
        scratch_shapes=[pltpu.VMEM((n_maps, tq, MIX_COLS), BF16)] + _dense_scratch(n_maps, tq, tk, defer),
        compiler_params=_cparams(("arbitrary", "arbitrary", "arbitrary")), name="attn_diff",
    )(q, kt, v, lam, g_tiled)


def _window_heads(q, kw, vw, add, mult, o_ref):
    head = _head_of_lane(q.shape)
    out = jnp.zeros(q.shape, F32)
    for hd in range(N_HEADS):
        qh = jnp.where(head == hd, q, jnp.zeros_like(q))
        s = _nt_dot(qh, kw)
        if add is not None:
            s = s + add(hd)
        if mult is not None:
            s = jnp.where(mult > 0.0, s, NEG_BIG)
        m = jnp.max(s, axis=1, keepdims=True)
        p = jnp.exp2(s - m)
        if mult is not None:
            p = p * mult
        l = jnp.sum(p, axis=1, keepdims=True)
        o = jnp.dot(p.astype(BF16), vw, preferred_element_type=F32)
        out = out + jnp.where(head == hd, o * (1.0 / l), 0.0)
    o_ref[...] = out.astype(o_ref.dtype)


def _nbr_kernel(nb, q_ref, k_ref, v_ref, bias_ref, o_ref):
    i = pl.program_id(1)
    start = pl.multiple_of(jnp.clip(i - 1, 0, nb - B_WIN // QBLK) * QBLK, QBLK)
    kw = k_ref[pl.ds(start, B_WIN), :]
    vw = v_ref[pl.ds(start, B_WIN), :]
    _window_heads(q_ref[...], kw, vw, lambda hd: bias_ref[0, hd], None, o_ref)


def _nbr_block_type(i, nb):
    return jnp.where(i < 1, 0, jnp.where(i > nb - 2, 2, 1))


def _nbr_call(q, k, v, bias, bsz, t_len):
    n = q.shape[0]
    nb = t_len // QBLK
    return pl.pallas_call(
        functools.partial(_nbr_kernel, nb), grid=(bsz, nb),
        in_specs=[pl.BlockSpec((QBLK, MIX_COLS), lambda b, i: (b * nb + i, 0)),
                  pl.BlockSpec((t_len, MIX_COLS), lambda b, i: (b, 0)),
                  pl.BlockSpec((t_len, MIX_COLS), lambda b, i: (b, 0)),
                  pl.BlockSpec((1, N_HEADS, QBLK, B_WIN), lambda b, i: (_nbr_block_type(i, nb), 0, 0, 0))],
        out_specs=pl.BlockSpec((QBLK, MIX_COLS), lambda b, i: (b * nb + i, 0)),
        out_shape=jax.ShapeDtypeStruct((n, MIX_COLS), BF16),
        compiler_params=_cparams(("parallel", "arbitrary")), name="attn_nbr",
    )(q, k, v, bias)


def _nbr_bias_tables(rpb, t_len):
    rows = t_len // GRID_W
    nb = t_len // QBLK
    kr = min(NA_ROWS, rows)
    reps = (0, 1, nb - 1)
    n_dr, n_dc = 2 * NA_ROWS - 1, 2 * NA_COLS - 1
    q_rows, w_rows = QBLK // GRID_W, B_WIN // GRID_W
    qc = np.arange(GRID_W)[:, None]
    kc = np.arange(GRID_W)[None, :]
    cs = np.clip(qc - NA_COLS // 2, 0, GRID_W - NA_COLS)
    col_ok = (kc >= cs) & (kc < cs + NA_COLS)
    sel_c = ((kc - qc + NA_COLS - 1)[None] == np.arange(n_dc)[:, None, None]) & col_ok[None]
    sel_r = np.zeros((len(reps), q_rows, w_rows, n_dr), np.float32)
    row_ok = np.zeros((len(reps), q_rows, w_rows), bool)
    for ti, b in enumerate(reps):
        start_row = int(np.clip(b - 1, 0, nb - B_WIN // QBLK)) * q_rows
        for qr in range(q_rows):
            r = b * q_rows + qr
            rs = int(np.clip(r - kr // 2, 0, rows - kr))
            for wr in range(w_rows):
                key_r = start_row + wr
                if rs <= key_r < rs + kr:
                    row_ok[ti, qr, wr] = True
                    sel_r[ti, qr, wr, key_r - r + NA_ROWS - 1] = 1.0
    hp = lax.Precision.HIGHEST
    by_col = jnp.einsum("hrd,dqk->hrqk", rpb.astype(F32), jnp.asarray(sel_c, F32), precision=hp)
    dense = jnp.einsum("tawr,hrqk->thaqwk", jnp.asarray(sel_r), by_col, precision=hp) * LOG2E
    ok = row_ok[:, None, :, None, :, None] & col_ok[None, None, None, :, None, :]
    return jnp.where(ok, dense, NEG_BIG).reshape(len(reps), N_HEADS, QBLK, B_WIN)


def _dil_kernel(t_len, tq, q_ref, k_ref, v_ref, mask_ref, o_ref):
    i = pl.program_id(1)
    w = tq + 2 * D_REACH
    start = pl.multiple_of(jnp.clip(i * tq - D_REACH, 0, t_len - w), tq)
    kw = k_ref[pl.ds(start, w), :]
    vw = v_ref[pl.ds(start, w), :]
    _window_heads(q_ref[...], kw, vw, lambda hd: mask_ref[0], None, o_ref)


def _dil_masks(tq):
    w = tq + 2 * D_REACH
    n_place = 2 * D_REACH // tq + 1
    place = jnp.arange(n_place, dtype=jnp.int32)[:, None, None] * tq
    d = (lax.broadcasted_iota(jnp.int32, (n_place, tq, w), 2) - place
         - lax.broadcasted_iota(jnp.int32, (n_place, tq, w), 1))
    ad = jnp.abs(d)
    count = ((ad <= 64).astype(F32)
             + (((d & 3) == 0) & (ad <= 256)).astype(F32)
             + (((d & 15) == 0) & (ad <= D_REACH)).astype(F32))
    return jnp.where(count > 0.0, jnp.log2(jnp.maximum(count, 1.0)), NEG_BIG)


def _dil_call(q, k, v, masks, bsz, t_len, tq):
    n = q.shape[0]
    nq = t_len // tq
    w = tq + 2 * D_REACH
    n_place = masks.shape[0]
    half = D_REACH // tq

    def placement(i):
        return jnp.where(i < half, i, jnp.where(i > nq - 1 - half, i - nq + n_place, half))

    whole = lambda: pl.BlockSpec((t_len, MIX_COLS), lambda b, i: (b, 0), pipeline_mode=pl.Buffered(1))
    return pl.pallas_call(
        functools.partial(_dil_kernel, t_len, tq), grid=(bsz, nq),
        in_specs=[pl.BlockSpec((tq, MIX_COLS), lambda b, i: (b * nq + i, 0)), whole(), whole(),
                  pl.BlockSpec((1, tq, w), lambda b, i: (placement(i), 0, 0))],
        out_specs=pl.BlockSpec((tq, MIX_COLS), lambda b, i: (b * nq + i, 0)),
        out_shape=jax.ShapeDtypeStruct((n, MIX_COLS), BF16),
        compiler_params=_cparams(("arbitrary", "arbitrary")), name="attn_dil",
    )(q, k, v, masks)


def _out_kernel(oa_ref, ob_ref, oc_ref, od_ref, x_ref, wo_ref, g2_ref, wr_ref, x2_ref, hn_ref, aff_ref):
    acc = x_ref[...]
    for mi, o_ref in enumerate((oa_ref, ob_ref, oc_ref, od_ref)):
        acc = acc + jnp.dot(o_ref[...], wo_ref[MIX_COLS * mi:MIX_COLS * (mi + 1), :], preferred_element_type=F32)
    x2_ref[...] = acc
    hn = _rms(acc) * g2_ref[...]
    hn_hi = hn.astype(BF16)
    hn_ref[...] = hn_hi
    hn_lo = (hn - hn_hi.astype(F32)).astype(BF16)
    logits = (jnp.dot(hn_hi, wr_ref[0], preferred_element_type=F32)
              + jnp.dot(hn_lo, wr_ref[0], preferred_element_type=F32)
              + jnp.dot(hn_hi, wr_ref[1], preferred_element_type=F32))
    lane = lax.broadcasted_iota(jnp.int32, logits.shape, 1)
    logits = jnp.where(lane < N_EXPERTS, logits, -jnp.inf)
    m = jnp.max(logits, axis=1, keepdims=True)
    e = jnp.exp(logits - m)
    aff_ref[...] = e / jnp.sum(e, axis=1, keepdims=True)


def _out_call(oa, ob, oc, od, x2d, wo, g2, wr, tm):
    n = x2d.shape[0]
    full = lambda a: pl.BlockSpec(a.shape, lambda i: (0,) * a.ndim)
    row = lambda w: pl.BlockSpec((tm, w), lambda i: (i, 0))
    return pl.pallas_call(
        _out_kernel, grid=(n // tm,),
        in_specs=[row(MIX_COLS)] * 4 + [row(D_MODEL), full(wo), full(g2), full(wr)],
        out_specs=(row(D_MODEL), row(D_MODEL), row(LANES)),
        out_shape=(jax.ShapeDtypeStruct((n, D_MODEL), F32), jax.ShapeDtypeStruct((n, D_MODEL), BF16),
                   jax.ShapeDtypeStruct((n, LANES), F32)),
        compiler_params=_cparams(("parallel",)), name="proj_out",
    )(oa, ob, oc, od, x2d, wo, g2, wr)


FLAG_FIRST, FLAG_LAST, FLAG_SUB = 1, 2, 4
Y_BLK = 128
GATHER_FANIN = 4
COMBINE_FANIN = 8


def _count_le(sorted_vals, queries):
    return jnp.sum((sorted_vals[None, :] <= queries[:, None]).astype(jnp.int32), axis=1)


def _step_lists(cnt, fanin, n_steps):
    steps = jnp.maximum((cnt + fanin - 1) // fanin, 1)
    ends = jnp.cumsum(steps)
    w = jnp.arange(n_steps, dtype=jnp.int32)
    grp = jnp.minimum(_count_le(ends, w), cnt.shape[0] - 1)
    per_grp = jnp.stack([ends - steps, steps, cnt], axis=1)[grp]
    rank = w - per_grp[:, 0]
    valid = w < ends[-1]
    item0 = rank * fanin
    flags = (jnp.where(valid & (rank == 0), FLAG_FIRST, 0)
             | jnp.where(valid & (rank == per_grp[:, 1] - 1), FLAG_LAST, 0))
    return grp, item0, flags, jnp.where(valid, per_grp[:, 2] - item0, 0)


def _with_live_flags(flags, remaining, slot_ranks):
    for k, r in enumerate(slot_ranks):
        flags = flags | jnp.where(r < remaining, FLAG_SUB << k, 0)
    return flags.astype(jnp.int32)


def _route_tables(aff, n, cap, ts, tc, tt):
    gate, idx = lax.top_k(aff[:, :N_EXPERTS].T, cap)
    idx, gate = lax.sort((idx, gate), dimension=1, num_keys=1)
    idx = idx.astype(jnp.int32)

    tiles_per_e = cap // ts
    n_tiles = N_EXPERTS * tiles_per_e
    n_chunks = n // tc
    c0 = (idx[:, ::ts] // tc).reshape(n_tiles)
    c1 = (idx[:, ts - 1::ts] // tc).reshape(n_tiles)
    n_gs = (N_EXPERTS * n_chunks + n_tiles) // GATHER_FANIN + n_tiles
    g_tile, g_item0, g_flags, g_left = _step_lists(c1 - c0 + 1, GATHER_FANIN, n_gs)
    g_first = c0[g_tile] + g_item0
    g_ranks = [(k - g_first) % GATHER_FANIN for k in range(GATHER_FANIN)]
    g_flags = _with_live_flags(g_flags, g_left, g_ranks)
    g_chunks = jnp.concatenate([jnp.minimum(g_first + r, n_chunks - 1) for r in g_ranks])

    n_tt = n // tt
    blk_per_e = cap // Y_BLK
    bounds = jnp.arange(n_tt + 1, dtype=jnp.int32) * tt
    pos = jnp.sum((idx[:, None, :] < bounds[None, :, None]).astype(jnp.int32), axis=2)
    lo, hi = pos[:, :-1], pos[:, 1:]
    b0 = jnp.minimum(lo // Y_BLK, blk_per_e - 1)
    b1 = jnp.maximum((hi - 1) // Y_BLK, b0)
    pair_cnt = jnp.where(hi > lo, b1 - b0 + 1, 0).T.reshape(-1)
    pair_b0 = (b0 + (jnp.arange(N_EXPERTS, dtype=jnp.int32) * blk_per_e)[:, None]).T.reshape(-1)
    n_items = N_EXPERTS * blk_per_e + N_EXPERTS * n_tt
    item_ends = jnp.cumsum(pair_cnt)
    it = jnp.arange(n_items, dtype=jnp.int32)
    pair = jnp.minimum(_count_le(item_ends, it), pair_cnt.shape[0] - 1)
    item_blk = jnp.minimum(pair_b0[pair] + it - (item_ends - pair_cnt)[pair], N_EXPERTS * blk_per_e - 1)
    tile_cnt = jnp.sum(pair_cnt.reshape(n_tt, N_EXPERTS), axis=1)
    tile_item0 = jnp.cumsum(tile_cnt) - tile_cnt
    n_cs = n_items // COMBINE_FANIN + n_tt
    c_tile, c_item0, c_flags, c_left = _step_lists(tile_cnt, COMBINE_FANIN, n_cs)
    c_flags = _with_live_flags(c_flags, c_left, list(range(COMBINE_FANIN)))
    c_first = tile_item0[c_tile] + c_item0
    c_blks = jnp.concatenate([item_blk[jnp.minimum(c_first + k, n_items - 1)] for k in range(COMBINE_FANIN)])
    return idx, gate, (g_tile, g_chunks, g_flags), (c_tile, c_blks, c_flags)


def _ffn_kernel(tc, tile_ref, chunk_ref, flag_ref, tok_ref, gate_ref, *refs):
    hn_refs = refs[:GATHER_FANIN]
    wg_ref, wu_ref, wd_ref, y_ref, xacc = refs[GATHER_FANIN:]
    w = pl.program_id(0)
    n_steps = pl.num_programs(0)
    flags = flag_ref[w]

    @pl.when((flags & FLAG_FIRST) != 0)
    def _():
        xacc[...] = jnp.zeros(xacc.shape, F32)

    for k in range(GATHER_FANIN):
        @pl.when((flags & (FLAG_SUB << k)) != 0)
        def _(k=k):
            ts = tok_ref.shape[0]
            token = chunk_ref[k * n_steps + w] * tc + lax.broadcasted_iota(jnp.int32, (ts, tc), 1)
            onehot = jnp.where(tok_ref[...] == token, 1.0, 0.0).astype(BF16)
            xacc[...] += jnp.dot(onehot, hn_refs[k][...], preferred_element_type=F32)

    @pl.when((flags & FLAG_LAST) != 0)
    def _():
        xg = xacc[...].astype(BF16)
        a = jnp.dot(xg, wg_ref[0, 0], preferred_element_type=F32)
        b = jnp.dot(xg, wu_ref[0, 0], preferred_element_type=F32)
        hid = (a * jax.nn.sigmoid(a) * b).astype(BF16)
        y_ref[...] = (jnp.dot(hid, wd_ref[0, 0], preferred_element_type=F32) * gate_ref[...]).astype(BF16)


def _ffn_call(steps, idx, gate, hn, w_gate, w_up, w_down, layer, ts, tc):
    n_exp, cap = idx.shape
    tiles_per_e = cap // ts
    tile, chunks, flags = steps
    n_steps = tile.shape[0]
    tok = idx.reshape(n_exp * cap, 1)
    gate2 = gate.reshape(n_exp * cap, 1)
    slot = lambda wd: pl.BlockSpec((ts, wd), lambda w, tile, chunks, flags: (tile[w], 0))
    chunk = lambda k: pl.BlockSpec((tc, D_MODEL), lambda w, tile, chunks, flags: (chunks[k * n_steps + w], 0))
    wspec = pl.BlockSpec((1, 1, D_MODEL, D_MODEL), lambda w, tile, chunks, flags: (layer, tile[w] // tiles_per_e, 0, 0))
    grid_spec = pltpu.PrefetchScalarGridSpec(
        num_scalar_prefetch=3, grid=(n_steps,),
        in_specs=[slot(1), slot(1)] + [chunk(k) for k in range(GATHER_FANIN)] + [wspec, wspec, wspec],
        out_specs=slot(D_MODEL),
        scratch_shapes=[pltpu.VMEM((ts, D_MODEL), F32)])
    return pl.pallas_call(
        functools.partial(_ffn_kernel, tc), grid_spec=grid_spec,
        out_shape=jax.ShapeDtypeStruct((n_exp * cap, D_MODEL), BF16),
        compiler_params=_cparams(("arbitrary",)), name="expert_ffn",
    )(tile, chunks, flags, tok, gate2, *([hn] * GATHER_FANIN), w_gate, w_up, w_down)


def _combine_kernel(tt, final, tile_ref, blk_ref, flag_ref, *refs):
    tok_refs = refs[:COMBINE_FANIN]
    y_refs = refs[COMBINE_FANIN:2 * COMBINE_FANIN]
    x_ref, g_ref, o_ref = refs[2 * COMBINE_FANIN:]
    w = pl.program_id(0)
    flags = flag_ref[w]

    @pl.when((flags & FLAG_FIRST) != 0)
    def _():
        o_ref[...] = x_ref[...]

    @pl.when((flags & FLAG_SUB) != 0)
    def _():
        token = tile_ref[w] * tt + lax.broadcasted_iota(jnp.int32, (tt, Y_BLK), 0)
        hots = []
        for k in range(COMBINE_FANIN):
            live = (flags & (FLAG_SUB << k)) != 0
            tok = jnp.where(live, tok_refs[k][0], -1)
            hots.append(jnp.where(token == tok, 1.0, 0.0).astype(BF16))
        onehot = jnp.concatenate(hots, axis=1)
        ycat = jnp.concatenate([y_refs[k][...] for k in range(COMBINE_FANIN)], axis=0)
        o_ref[...] += jnp.dot(onehot, ycat, preferred_element_type=F32)

    if final:
        @pl.when((flags & FLAG_LAST) != 0)
        def _():
            o_ref[...] = _rms(o_ref[...]) * g_ref[...]


def _combine_call(steps, idx, y, x2, g_final, tt, final):
    n = x2.shape[0]
    tile, blks, flags = steps
    n_steps = tile.shape[0]
    tok = idx.reshape(-1, 1, Y_BLK)
    tspec = lambda k: pl.BlockSpec((1, 1, Y_BLK), lambda w, tile, blks, flags: (blks[k * n_steps + w], 0, 0))
    yspec = lambda k: pl.BlockSpec((Y_BLK, D_MODEL), lambda w, tile, blks, flags: (blks[k * n_steps + w], 0))
    xspec = pl.BlockSpec((tt, D_MODEL), lambda w, tile, blks, flags: (tile[w], 0))
    fan = range(COMBINE_FANIN)
    grid_spec = pltpu.PrefetchScalarGridSpec(
        num_scalar_prefetch=3, grid=(n_steps,),
        in_specs=[tspec(k) for k in fan] + [yspec(k) for k in fan]
        + [xspec, pl.BlockSpec((1, D_MODEL), lambda w, tile, blks, flags: (0, 0))],
        out_specs=xspec)
    return pl.pallas_call(
        functools.partial(_combine_kernel, tt, final), grid_spec=grid_spec,
        out_shape=jax.ShapeDtypeStruct((n, D_MODEL), F32),
        compiler_params=_cparams(("arbitrary",)), name="expert_combine",
    )(tile, blks, flags, *([tok] * COMBINE_FANIN), *([y] * COMBINE_FANIN), x2, g_final)


def _rot_cols(w, d):
    k, c = w.shape
    half = d // 2
    wg = w.reshape(k, c // d, 2, half)
    return jnp.concatenate([-wg[:, :, 1], wg[:, :, 0]], axis=2).reshape(k, c)


def _rope_tables(t_len):
    pos = jnp.arange(t_len, dtype=F32)

    def cs(d):
        half = d // 2
        inv = ROPE_THETA ** (-jnp.arange(half, dtype=F32) / half)
        ang = pos[:, None] * inv[None, :]
        return (jnp.concatenate([jnp.cos(ang)] * 2, axis=1), jnp.concatenate([jnp.sin(ang)] * 2, axis=1))

    c32, s32 = cs(A_ROPE)
    c64, s64 = cs(HEAD_DIM)
    ones = jnp.ones((t_len, A_NOPE), F32)
    zeros = jnp.zeros((t_len, A_NOPE), F32)
    pad = jnp.zeros((t_len, A_PAD - A_NOPE - A_ROPE), F32)
    tab = jnp.concatenate([ones, c32, pad, zeros, s32, pad,
                           jnp.tile(c32, (1, LANES // A_ROPE)), jnp.tile(s32, (1, LANES // A_ROPE)),
                           jnp.tile(c64, (1, LANES // HEAD_DIM)), jnp.tile(s64, (1, LANES // HEAD_DIM))], axis=1)
    tabt = jnp.concatenate([c32.T, s32.T], axis=0)
    return tab, tabt


def _layer_weights(l, w_in, a_w_uq, a_w_ukv, w_out, w_router):
    pts = np.cumsum(PROJ_SIZES)[:-1]
    (a_cq, a_ckv, a_kr, b_q, b_k, b_v, c_q, c_k, c_v, d_q, d_k, d_v) = jnp.split(w_in[l], pts, axis=1)
    wm = jnp.concatenate([a_cq, a_ckv, b_q, b_k, b_v, c_q, _rot_cols(c_q, C_DIM), c_v,
                          d_q, _rot_cols(d_q, HEAD_DIM), d_k, _rot_cols(d_k, HEAD_DIM), d_v], axis=1).astype(BF16)
    wt = jnp.concatenate([c_k, _rot_cols(c_k, C_DIM), a_kr, _rot_cols(a_kr, A_ROPE)], axis=1).T.astype(BF16)
    uq = a_w_uq[l].reshape(A_Q_LORA, N_HEADS, A_NOPE + A_ROPE)
    zpad = jnp.zeros((A_Q_LORA, N_HEADS, A_PAD - A_NOPE - A_ROPE), F32)
    uq_rope = uq[:, :, A_NOPE:]
    uq_rot = _rot_cols(uq_rope.reshape(A_Q_LORA, N_HEADS * A_ROPE), A_ROPE).reshape(A_Q_LORA, N_HEADS, A_ROPE)
    plain = jnp.concatenate([uq, zpad], axis=2).reshape(A_Q_LORA, N_HEADS * A_PAD)
    rot = jnp.concatenate([jnp.zeros_like(uq[:, :, :A_NOPE]), uq_rot, zpad], axis=2).reshape(A_Q_LORA, N_HEADS * A_PAD)
    wuq = jnp.concatenate([plain, rot], axis=1).astype(BF16)
    ukv = a_w_ukv[l].reshape(A_KV_LORA, N_HEADS, 2 * HEAD_DIM)
    wukt = ukv[:, :, :A_NOPE].reshape(A_KV_LORA, MIX_COLS).T.astype(BF16)
    wuv = ukv[:, :, A_NOPE:].reshape(A_KV_LORA, MIX_COLS).astype(BF16)
    wo = w_out[l].astype(BF16)
    wr = jnp.concatenate([w_router[l], jnp.zeros((D_MODEL, LANES - N_EXPERTS), F32)], axis=1)
    wr_hi = wr.astype(BF16)
    wr = jnp.stack([wr_hi, (wr - wr_hi.astype(F32)).astype(BF16)])
    return wm, wt, wuq, wukt, wuv, wo, wr


def _trunk(x, norm1_g, w_in, a_qnorm_g, a_w_uq, a_kvnorm_g, a_w_ukv, b_rpb, c_lambda, c_subln_g, w_out,
           norm2_g, w_router, w_gate, w_up, w_down, final_g):
    bsz, t_len, _ = x.shape
    n = bsz * t_len
    tm = 512
    tq = 512
    tk_mla, tk_diff = min(4096, t_len), (4096 if t_len <= 4096 else 2048)
    tq_dil = 256
    ts, tc, tt = 256, 1024, 512
    depth = w_in.shape[0]
    cap = CAP_FACTOR * n // N_EXPERTS
    assert t_len % tk_mla == 0 and t_len % tk_diff == 0 and t_len >= tq_dil + 2 * D_REACH and t_len // QBLK >= 3
    assert cap % ts == 0 and n % tc == 0 and n % tt == 0
    tab, tabt = _rope_tables(t_len)
    dil_masks = _dil_masks(tq_dil)
    x2d = x.reshape(n, D_MODEL)
    g_final = final_g.reshape(1, D_MODEL)
    for l in range(depth):
        lam_init = 0.8 - 0.6 * math.exp(-0.3 * l)
        wm, wt, wuq, wukt, wuv, wo, wr = _layer_weights(l, w_in, a_w_uq, a_w_ukv, w_out, w_router)
        row = lambda g: g[l].reshape(1, -1)
        (qa, kat, va, qb, kb, vb, qc, kct, vc, qd, kd, vd) = _proj_call(
            x2d, t_len, row(norm1_g), wm, wt, row(a_qnorm_g), wuq, row(a_kvnorm_g), wukt, wuv, tab, tabt, tm)
        oa = _mla_call(qa, kat, va, bsz, t_len, tq, tk_mla)
        ob = _nbr_call(qb, kb, vb, _nbr_bias_tables(b_rpb[l], t_len), bsz, t_len)
        lp = c_lambda[l].astype(F32)
        lam = (jnp.exp(jnp.sum(lp[0] * lp[1])) - jnp.exp(jnp.sum(lp[2] * lp[3])) + lam_init).reshape(1, 1)
        g_sub = jnp.tile(c_subln_g[l], N_HEADS).reshape(1, MIX_COLS)
        oc = _diff_call(qc, kct, vc, lam, g_sub, lam_init, bsz, t_len, tq, tk_diff)
        od = _dil_call(qd, kd, vd, dil_masks, bsz, t_len, tq_dil)
        x2, hn, aff = _out_call(oa, ob, oc, od, x2d, wo, row(norm2_g), wr, tm)
        idx, gate, g_items, c_items = _route_tables(aff, n, cap, ts, tc, tt)
        y = _ffn_call(g_items, idx, gate, hn, w_gate, w_up, w_down, l, ts, tc)
        x2d = _combine_call(c_items, idx, y, x2, g_final, tt, l == depth - 1)
    return x2d.reshape(bsz, t_len, D_MODEL)


def kernel(x_prompt, x_sample, norm1_g, w_in, a_qnorm_g, a_w_uq, a_kvnorm_g, a_w_ukv, b_rpb, c_lambda, c_subln_g,
           w_out, norm2_g, w_router, w_gate, w_up, w_down, final_g):
    params = (norm1_g, w_in, a_qnorm_g, a_w_uq, a_kvnorm_g, a_w_ukv, b_rpb, c_lambda, c_subln_g, w_out,
              norm2_g, w_router, w_gate.astype(BF16), w_up.astype(BF16), w_down.astype(BF16), final_g)
    return (_trunk(x_prompt, *params), _trunk(x_sample, *params))
```

```python
import functools
import math

import numpy as np
import jax
import jax.numpy as jnp
from jax import lax
from jax.experimental import pallas as pl
from jax.experimental.pallas import tpu as pltpu

F32 = jnp.float32
BF16 = jnp.bfloat16

D_MODEL = 1024
N_HEADS = 4
HEAD_DIM = 64
MIX_COLS = N_HEADS * HEAD_DIM
A_NOPE = 64
A_ROPE = 32
A_PAD = 128
A_Q_LORA = 256
A_KV_LORA = 128
C_DIM = 32
GRID_W = 64
NA_ROWS = 8
NA_COLS = 16
QBLK = 256
B_WIN = 3 * QBLK
D_REACH = 1024
N_EXPERTS = 16
CAP_FACTOR = 2
ROPE_THETA = 10000.0
RMS_EPS = 1e-6
NEG_BIG = -1e30
LOG2E = math.log2(math.e)
LANES = 128
V7X_VMEM_LIMIT = 56 * 1024 * 1024

PROJ_SIZES = (256, 128, 32, 256, 256, 256, 256, 256, 256, 256, 256, 256)

_M_ACQ, _M_ACKV, _M_BQ, _M_BK, _M_BV = 0, 256, 384, 640, 896
_M_CQ, _M_CQR, _M_CV = 1152, 1408, 1664
_M_DQ, _M_DQR, _M_DK, _M_DKR, _M_DV = 1920, 2176, 2432, 2688, 2944
_M_COLS = 3200
_T_CK, _T_CKR, _T_AKR, _T_AKRR, _T_ROWS = 0, 256, 512, 544, 576


def _cparams(sem):
    return pltpu.CompilerParams(dimension_semantics=sem, vmem_limit_bytes=V7X_VMEM_LIMIT)


def _rms(x):
    return x * lax.rsqrt(jnp.mean(x * x, axis=-1, keepdims=True) + RMS_EPS)


def _nt_dot(a, b):
    return lax.dot_general(a, b, (((1,), (1,)), ((), ())), preferred_element_type=F32)


def _head_of_lane(shape):
    return lax.broadcasted_iota(jnp.int32, shape, len(shape) - 1) // HEAD_DIM


def _proj_kernel(x_ref, g1_ref, wm_ref, wt_ref, gq_ref, wuq_ref, gkv_ref, wukt_ref, wuv_ref,
                 tab_ref, tabt_ref,
                 qa_ref, kat_ref, va_ref, qb_ref, kb_ref, vb_ref, qc_ref, kct_ref, vc_ref,
                 qd_ref, kd_ref, vd_ref):
    x = x_ref[...]
    h = (_rms(x) * g1_ref[...]).astype(BF16)
    p = jnp.dot(h, wm_ref[...], preferred_element_type=F32)
    pt = _nt_dot(wt_ref[...], h)
    tab = tab_ref[...]
    cos_a, sin_a, cos_c, sin_c, cos_d, sin_d = (tab[:, LANES * i:LANES * (i + 1)] for i in range(6))
    tabt = tabt_ref[...]
    cos_t, sin_t = tabt[0:A_ROPE], tabt[A_ROPE:2 * A_ROPE]

    scale_a = LOG2E * (A_NOPE + A_ROPE) ** -0.5
    latq = (_rms(p[:, _M_ACQ:_M_ACQ + A_Q_LORA]) * gq_ref[...]).astype(BF16)
    qa2 = jnp.dot(latq, wuq_ref[...], preferred_element_type=F32)
    for hd in range(N_HEADS):
        lo = A_PAD * hd
        blk = qa2[:, lo:lo + A_PAD] * cos_a + qa2[:, N_HEADS * A_PAD + lo:N_HEADS * A_PAD + lo + A_PAD] * sin_a
        qa_ref[:, lo:lo + A_PAD] = (blk * scale_a).astype(BF16)
    latkv = (_rms(p[:, _M_ACKV:_M_ACKV + A_KV_LORA]) * gkv_ref[...]).astype(BF16)
    va_ref[...] = jnp.dot(latkv, wuv_ref[...], preferred_element_type=F32).astype(BF16)
    knt = _nt_dot(wukt_ref[...], latkv)
    krt = (pt[_T_AKR:_T_AKR + A_ROPE] * cos_t + pt[_T_AKRR:_T_AKRR + A_ROPE] * sin_t).astype(BF16)
    tm = x.shape[0]
    for hd in range(N_HEADS):
        lo = A_PAD * hd
        kat_ref[lo:lo + A_NOPE, :] = knt[A_NOPE * hd:A_NOPE * (hd + 1)].astype(BF16)
        kat_ref[lo + A_NOPE:lo + A_NOPE + A_ROPE, :] = krt
        kat_ref[lo + A_NOPE + A_ROPE:lo + A_PAD, :] = jnp.zeros((A_PAD - A_NOPE - A_ROPE, tm), BF16)

    qb_ref[...] = (p[:, _M_BQ:_M_BQ + MIX_COLS] * (LOG2E * HEAD_DIM ** -0.5)).astype(BF16)
    kb_ref[...] = p[:, _M_BK:_M_BK + MIX_COLS].astype(BF16)
    vb_ref[...] = p[:, _M_BV:_M_BV + MIX_COLS].astype(BF16)

    scale_c = LOG2E * C_DIM ** -0.5
    for j in range(MIX_COLS // LANES):
        lo = LANES * j
        blk = p[:, _M_CQ + lo:_M_CQ + lo + LANES] * cos_c + p[:, _M_CQR + lo:_M_CQR + lo + LANES] * sin_c
        qc_ref[:, lo:lo + LANES] = (blk * scale_c).astype(BF16)
    reps = MIX_COLS // A_ROPE
    cos_ct = jnp.concatenate([cos_t] * reps, axis=0)
    sin_ct = jnp.concatenate([sin_t] * reps, axis=0)
    kct_ref[...] = (pt[_T_CK:_T_CK + MIX_COLS] * cos_ct + pt[_T_CKR:_T_CKR + MIX_COLS] * sin_ct).astype(BF16)
    vc_ref[...] = p[:, _M_CV:_M_CV + MIX_COLS].astype(BF16)

    for j in range(MIX_COLS // LANES):
        lo = LANES * j
        qblk = p[:, _M_DQ + lo:_M_DQ + lo + LANES] * cos_d + p[:, _M_DQR + lo:_M_DQR + lo + LANES] * sin_d
        qd_ref[:, lo:lo + LANES] = (qblk * (LOG2E * HEAD_DIM ** -0.5)).astype(BF16)
        kblk = p[:, _M_DK + lo:_M_DK + lo + LANES] * cos_d + p[:, _M_DKR + lo:_M_DKR + lo + LANES] * sin_d
        kd_ref[:, lo:lo + LANES] = kblk.astype(BF16)
    vd_ref[...] = p[:, _M_DV:_M_DV + MIX_COLS].astype(BF16)


def _proj_call(x2d, t_len, g1, wm, wt, gq, wuq, gkv, wukt, wuv, tab, tabt, tm):
    n = x2d.shape[0]
    nt = t_len // tm
    full = lambda a: pl.BlockSpec(a.shape, lambda i: (0,) * a.ndim)
    row = lambda w: pl.BlockSpec((tm, w), lambda i: (i, 0))
    col = lambda r: pl.BlockSpec((r, tm), lambda i: (0, i))
    tok = lambda w: jax.ShapeDtypeStruct((n, w), BF16)
    out_shape = (tok(N_HEADS * A_PAD), jax.ShapeDtypeStruct((N_HEADS * A_PAD, n), BF16), tok(MIX_COLS),
                 tok(MIX_COLS), tok(MIX_COLS), tok(MIX_COLS),
                 tok(MIX_COLS), jax.ShapeDtypeStruct((MIX_COLS, n), BF16), tok(MIX_COLS),
                 tok(MIX_COLS), tok(MIX_COLS), tok(MIX_COLS))
    out_specs = (row(N_HEADS * A_PAD), col(N_HEADS * A_PAD), row(MIX_COLS),
                 row(MIX_COLS), row(MIX_COLS), row(MIX_COLS),
                 row(MIX_COLS), col(MIX_COLS), row(MIX_COLS),
                 row(MIX_COLS), row(MIX_COLS), row(MIX_COLS))
    in_specs = [row(D_MODEL), full(g1), full(wm), full(wt), full(gq), full(wuq), full(gkv), full(wukt), full(wuv),
                pl.BlockSpec((tm, tab.shape[1]), lambda i: (i % nt, 0)),
                pl.BlockSpec((tabt.shape[0], tm), lambda i: (0, i % nt))]
    return pl.pallas_call(
        _proj_kernel, grid=(n // tm,), in_specs=in_specs, out_specs=out_specs, out_shape=out_shape,
        compiler_params=_cparams(("parallel",)), name="proj_in",
    )(x2d, g1, wm, wt, gq, wuq, gkv, wukt, wuv, tab, tabt)


def _row_total(l_lanes):
    return jnp.sum(l_lanes, axis=1, keepdims=True)


def _online_step(s, v, m_ref, l_ref, acc_ref, idx, first, keep=None):
    m_prev = jnp.where(first, -jnp.inf, m_ref[idx])
    l_prev = jnp.where(first, 0.0, l_ref[idx])
    acc_prev = jnp.where(first, 0.0, acc_ref[idx])
    m_new = jnp.maximum(m_prev, jnp.max(s, axis=1, keepdims=True))
    alpha = jnp.exp2(m_prev - m_new)
    p = jnp.exp2(s - m_new)
    part = p[:, 0:LANES]
    for c in range(1, s.shape[1] // LANES):
        part = part + p[:, LANES * c:LANES * (c + 1)]
    l_new = alpha * l_prev + part
    acc_new = alpha * acc_prev + jnp.dot(p.astype(BF16), v, preferred_element_type=F32)
    if keep is not None:
        l_new, acc_new = jnp.where(keep, l_new, 0.0), jnp.where(keep, acc_new, 0.0)
        m_new = jnp.where(keep, m_new, -jnp.inf)
    l_ref[idx] = l_new
    acc_ref[idx] = acc_new
    m_ref[idx] = m_new


def _zero_state_once(refs):
    @pl.when((pl.program_id(0) == 0) & (pl.program_id(1) == 0) & (pl.program_id(2) == 0))
    def _():
        for r in refs:
            r[...] = jnp.zeros(r.shape, r.dtype)


def _dense_maps(n_maps, defer, score, v_block, m_ref, l_ref, acc_ref, s_ref, finish):
    j = pl.program_id(2)
    first = j == 0
    is_last = j == pl.num_programs(2) - 1
    _zero_state_once([m_ref, l_ref, acc_ref] + ([s_ref] if defer else []))
    if defer:
        keep = jnp.logical_not(first)
        _online_step(s_ref[...], v_block(jnp.maximum(j - 1, 0)), m_ref, l_ref, acc_ref, n_maps - 1, first, keep)
    v = v_block(j)
    for mi in range(n_maps - 1 if defer else n_maps):
        _online_step(score(mi), v, m_ref, l_ref, acc_ref, mi, first)
    if defer:
        s_ref[...] = score(n_maps - 1)

    @pl.when(is_last)
    def _():
        if defer:
            _online_step(s_ref[...], v_block(j), m_ref, l_ref, acc_ref, n_maps - 1, False)
        finish()


def _value_block(v_ref, tk):
    return lambda j: v_ref[pl.ds(pl.multiple_of(j * tk, tk), tk), :]


def _mla_kernel(defer, q_ref, kt_ref, v_ref, o_ref, m_ref, l_ref, acc_ref, *s_ref):
    s_ref = s_ref[0] if defer else None

    def score(hd):
        lo = A_PAD * hd
        return jnp.dot(q_ref[:, lo:lo + A_PAD], kt_ref[lo:lo + A_PAD, :], preferred_element_type=F32)

    def finish():
        head = _head_of_lane(o_ref.shape)
        out = jnp.zeros(o_ref.shape, F32)
        for hd in range(N_HEADS):
            out = out + jnp.where(head == hd, acc_ref[hd] * (1.0 / _row_total(l_ref[hd])), 0.0)
        o_ref[...] = out.astype(o_ref.dtype)

    _dense_maps(N_HEADS, defer, score, _value_block(v_ref, kt_ref.shape[1]), m_ref, l_ref, acc_ref, s_ref, finish)


def _dense_specs(bsz, t_len, tq, tk, q_cols, k_rows):
    nq, nk = t_len // tq, t_len // tk
    specs = [pl.BlockSpec((tq, q_cols), lambda b, i, j: (b * nq + i, 0)),
             pl.BlockSpec((k_rows, tk), lambda b, i, j: (0, b * nk + j)),
             pl.BlockSpec((t_len, MIX_COLS), lambda b, i, j: (b, 0), pipeline_mode=pl.Buffered(1))]
    out_spec = pl.BlockSpec((tq, MIX_COLS), lambda b, i, j: (b * nq + i, 0))
    return (bsz, nq, nk), specs, out_spec


def _dense_scratch(n_maps, tq, tk, defer):
    shapes = [pltpu.VMEM((n_maps, tq, 1), F32), pltpu.VMEM((n_maps, tq, LANES), F32),
              pltpu.VMEM((n_maps, tq, MIX_COLS), F32)]
    return shapes + ([pltpu.VMEM((tq, tk), F32)] if defer else [])


def _mla_call(q, kt, v, bsz, t_len, tq, tk):
    n = q.shape[0]
    defer = t_len // tk > 2
    grid, in_specs, out_spec = _dense_specs(bsz, t_len, tq, tk, N_HEADS * A_PAD, N_HEADS * A_PAD)
    return pl.pallas_call(
        functools.partial(_mla_kernel, defer), grid=grid, in_specs=in_specs, out_specs=out_spec,
        out_shape=jax.ShapeDtypeStruct((n, MIX_COLS), BF16),
        scratch_shapes=_dense_scratch(N_HEADS, tq, tk, defer),
        compiler_params=_cparams(("arbitrary", "arbitrary", "arbitrary")), name="attn_mla",
    )(q, kt, v)


def _diff_kernel(lam_init, defer, q_ref, kt_ref, v_ref, lam_ref, g_ref, o_ref, qm_ref, m_ref, l_ref, acc_ref, *s_ref):
    s_ref = s_ref[0] if defer else None
    n_maps = 2 * N_HEADS

    @pl.when(pl.program_id(2) == 0)
    def _():
        q = q_ref[...]
        group = lax.broadcasted_iota(jnp.int32, q.shape, 1) // C_DIM
        for mi in range(n_maps):
            qm_ref[mi] = jnp.where(group == mi, q, jnp.zeros_like(q))

    def score(mi):
        return jnp.dot(qm_ref[mi], kt_ref[...], preferred_element_type=F32)

    def finish():
        lam = lam_ref[...]
        head = _head_of_lane(o_ref.shape)
        o = jnp.zeros(o_ref.shape, F32)
        for hd in range(N_HEADS):
            oh = (acc_ref[2 * hd] * (1.0 / _row_total(l_ref[2 * hd]))
                  - lam * (acc_ref[2 * hd + 1] * (1.0 / _row_total(l_ref[2 * hd + 1]))))
            o = o + jnp.where(head == hd, oh, 0.0)
        o2 = o * o
        inv = jnp.zeros(o_ref.shape, F32)
        for hd in range(N_HEADS):
            ms = jnp.sum(jnp.where(head == hd, o2, 0.0), axis=1, keepdims=True) * (1.0 / HEAD_DIM)
            inv = inv + jnp.where(head == hd, lax.rsqrt(ms + RMS_EPS), 0.0)
        o_ref[...] = ((o * inv * g_ref[...]) * (1.0 - lam_init)).astype(o_ref.dtype)

    _dense_maps(n_maps, defer, score, _value_block(v_ref, kt_ref.shape[1]), m_ref, l_ref, acc_ref, s_ref, finish)


def _diff_call(q, kt, v, lam, g_tiled, lam_init, bsz, t_len, tq, tk):
    n = q.shape[0]
    n_maps = 2 * N_HEADS
    defer = t_len // tk > 2
    grid, in_specs, out_spec = _dense_specs(bsz, t_len, tq, tk, MIX_COLS, MIX_COLS)
    in_specs += [pl.BlockSpec((1, 1), lambda b, i, j: (0, 0)), pl.BlockSpec((1, MIX_COLS), lambda b, i, j: (0, 0))]
    return pl.pallas_call(
        functools.partial(_diff_kernel, lam_init, defer), grid=grid, in_specs=in_specs, out_specs=out_spec,
        out_shape=jax.ShapeDtypeStruct((n, MIX_COLS), BF16),
        scratch_shapes=[pltpu.VMEM((n_maps, tq, MIX_COLS), BF16)] + _dense_scratch(n_maps, tq, tk, defer),
        compiler_params=_cparams(("arbitrary", "arbitrary", "arbitrary")), name="attn_diff",
    )(q, kt, v, lam, g_tiled)


def _window_heads(q, kw, vw, add, mult, o_ref):
    head = _head_of_lane(q.shape)
    out = jnp.zeros(q.shape, F32)
    for hd in range(N_HEADS):
        qh = jnp.where(head == hd, q, jnp.zeros_like(q))
        s = _nt_dot(qh, kw)
        if add is not None:
            s = s + add(hd)
        if mult is not None:
            s = jnp.where(mult > 0.0, s, NEG_BIG)
        m = jnp.max(s, axis=1, keepdims=True)
        p = jnp.exp2(s - m)
        if mult is not None:
            p = p * mult
        l = jnp.sum(p, axis=1, keepdims=True)
        o = jnp.dot(p.astype(BF16), vw, preferred_element_type=F32)
        out = out + jnp.where(head == hd, o * (1.0 / l), 0.0)
    o_ref[...] = out.astype(o_ref.dtype)


def _nbr_kernel(nb, q_ref, k_ref, v_ref, bias_ref, o_ref):
    i = pl.program_id(1)
    start = pl.multiple_of(jnp.clip(i - 1, 0, nb - B_WIN // QBLK) * QBLK, QBLK)
    kw = k_ref[pl.ds(start, B_WIN), :]
    vw = v_ref[pl.ds(start, B_WIN), :]
    _window_heads(q_ref[...], kw, vw, lambda hd: bias_ref[0, hd], None, o_ref)


def _nbr_block_type(i, nb):
    return jnp.where(i < 1, 0, jnp.where(i > nb - 2, 2, 1))


def _nbr_call(q, k, v, bias, bsz, t_len):
    n = q.shape[0]
    nb = t_len // QBLK
    return pl.pallas_call(
        functools.partial(_nbr_kernel, nb), grid=(bsz, nb),
        in_specs=[pl.BlockSpec((QBLK, MIX_COLS), lambda b, i: (b * nb + i, 0)),
                  pl.BlockSpec((t_len, MIX_COLS), lambda b, i: (b, 0)),
                  pl.BlockSpec((t_len, MIX_COLS), lambda b, i: (b, 0)),
                  pl.BlockSpec((1, N_HEADS, QBLK, B_WIN), lambda b, i: (_nbr_block_type(i, nb), 0, 0, 0))],
        out_specs=pl.BlockSpec((QBLK, MIX_COLS), lambda b, i: (b * nb + i, 0)),
        out_shape=jax.ShapeDtypeStruct((n, MIX_COLS), BF16),
        compiler_params=_cparams(("parallel", "arbitrary")), name="attn_nbr",
    )(q, k, v, bias)


def _nbr_bias_tables(rpb, t_len):
    rows = t_len // GRID_W
    nb = t_len // QBLK
    kr = min(NA_ROWS, rows)
    reps = (0, 1, nb - 1)
    n_dr, n_dc = 2 * NA_ROWS - 1, 2 * NA_COLS - 1
    q_rows, w_rows = QBLK // GRID_W, B_WIN // GRID_W
    qc = np.arange(GRID_W)[:, None]
    kc = np.arange(GRID_W)[None, :]
    cs = np.clip(qc - NA_COLS // 2, 0, GRID_W - NA_COLS)
    col_ok = (kc >= cs) & (kc < cs + NA_COLS)
    sel_c = ((kc - qc + NA_COLS - 1)[None] == np.arange(n_dc)[:, None, None]) & col_ok[None]
    sel_r = np.zeros((len(reps), q_rows, w_rows, n_dr), np.float32)
    row_ok = np.zeros((len(reps), q_rows, w_rows), bool)
    for ti, b in enumerate(reps):
        start_row = int(np.clip(b - 1, 0, nb - B_WIN // QBLK)) * q_rows
        for qr in range(q_rows):
            r = b * q_rows + qr
            rs = int(np.clip(r - kr // 2, 0, rows - kr))
            for wr in range(w_rows):
                key_r = start_row + wr
                if rs <= key_r < rs + kr:
                    row_ok[ti, qr, wr] = True
                    sel_r[ti, qr, wr, key_r - r + NA_ROWS - 1] = 1.0
    hp = lax.Precision.HIGHEST
    by_col = jnp.einsum("hrd,dqk->hrqk", rpb.astype(F32), jnp.asarray(sel_c, F32), precision=hp)
    dense = jnp.einsum("tawr,hrqk->thaqwk", jnp.asarray(sel_r), by_col, precision=hp) * LOG2E
    ok = row_ok[:, None, :, None, :, None] & col_ok[None, None, None, :, None, :]
    return jnp.where(ok, dense, NEG_BIG).reshape(len(reps), N_HEADS, QBLK, B_WIN)


def _dil_kernel(t_len, tq, q_ref, k_ref, v_ref, mask_ref, o_ref):
    i = pl.program_id(1)
    w = tq + 2 * D_REACH
    start = pl.multiple_of(jnp.clip(i * tq - D_REACH, 0, t_len - w), tq)
    kw = k_ref[pl.ds(start, w), :]
    vw = v_ref[pl.ds(start, w), :]
    _window_heads(q_ref[...], kw, vw, lambda hd: mask_ref[0], None, o_ref)


def _dil_masks(tq):
    w = tq + 2 * D_REACH
    n_place = 2 * D_REACH // tq + 1
    place = jnp.arange(n_place, dtype=jnp.int32)[:, None, None] * tq
    d = (lax.broadcasted_iota(jnp.int32, (n_place, tq, w), 2) - place
         - lax.broadcasted_iota(jnp.int32, (n_place, tq, w), 1))
    ad = jnp.abs(d)
    count = ((ad <= 64).astype(F32)
             + (((d & 3) == 0) & (ad <= 256)).astype(F32)
             + (((d & 15) == 0) & (ad <= D_REACH)).astype(F32))
    return jnp.where(count > 0.0, jnp.log2(jnp.maximum(count, 1.0)), NEG_BIG)


def _dil_call(q, k, v, masks, bsz, t_len, tq):
    n = q.shape[0]
    nq = t_len // tq
    w = tq + 2 * D_REACH
    n_place = masks.shape[0]
    half = D_REACH // tq

    def placement(i):
        return jnp.where(i < half, i, jnp.where(i > nq - 1 - half, i - nq + n_place, half))

    whole = lambda: pl.BlockSpec((t_len, MIX_COLS), lambda b, i: (b, 0), pipeline_mode=pl.Buffered(1))
    return pl.pallas_call(
        functools.partial(_dil_kernel, t_len, tq), grid=(bsz, nq),
        in_specs=[pl.BlockSpec((tq, MIX_COLS), lambda b, i: (b * nq + i, 0)), whole(), whole(),
                  pl.BlockSpec((1, tq, w), lambda b, i: (placement(i), 0, 0))],
        out_specs=pl.BlockSpec((tq, MIX_COLS), lambda b, i: (b * nq + i, 0)),
        out_shape=jax.ShapeDtypeStruct((n, MIX_COLS), BF16),
        compiler_params=_cparams(("arbitrary", "arbitrary")), name="attn_dil",
    )(q, k, v, masks)


def _out_kernel(oa_ref, ob_ref, oc_ref, od_ref, x_ref, wo_ref, g2_ref, wr_ref, x2_ref, hn_ref, aff_ref):
    acc = x_ref[...]
    for mi, o_ref in enumerate((oa_ref, ob_ref, oc_ref, od_ref)):
        acc = acc + jnp.dot(o_ref[...], wo_ref[MIX_COLS * mi:MIX_COLS * (mi + 1), :], preferred_element_type=F32)
    x2_ref[...] = acc
    hn = _rms(acc) * g2_ref[...]
    hn_hi = hn.astype(BF16)
    hn_ref[...] = hn_hi
    hn_lo = (hn - hn_hi.astype(F32)).astype(BF16)
    logits = (jnp.dot(hn_hi, wr_ref[0], preferred_element_type=F32)
              + jnp.dot(hn_lo, wr_ref[0], preferred_element_type=F32)
              + jnp.dot(hn_hi, wr_ref[1], preferred_element_type=F32))
    lane = lax.broadcasted_iota(jnp.int32, logits.shape, 1)
    logits = jnp.where(lane < N_EXPERTS, logits, -jnp.inf)
    m = jnp.max(logits, axis=1, keepdims=True)
    e = jnp.exp(logits - m)
    aff_ref[...] = e / jnp.sum(e, axis=1, keepdims=True)


def _out_call(oa, ob, oc, od, x2d, wo, g2, wr, tm):
    n = x2d.shape[0]
    full = lambda a: pl.BlockSpec(a.shape, lambda i: (0,) * a.ndim)
    row = lambda w: pl.BlockSpec((tm, w), lambda i: (i, 0))
    return pl.pallas_call(
        _out_kernel, grid=(n // tm,),
        in_specs=[row(MIX_COLS)] * 4 + [row(D_MODEL), full(wo), full(g2), full(wr)],
        out_specs=(row(D_MODEL), row(D_MODEL), row(LANES)),
        out_shape=(jax.ShapeDtypeStruct((n, D_MODEL), F32), jax.ShapeDtypeStruct((n, D_MODEL), BF16),
                   jax.ShapeDtypeStruct((n, LANES), F32)),
        compiler_params=_cparams(("parallel",)), name="proj_out",
    )(oa, ob, oc, od, x2d, wo, g2, wr)


FLAG_FIRST, FLAG_LAST, FLAG_SUB = 1, 2, 4
Y_BLK = 128
GATHER_FANIN = 4
COMBINE_FANIN = 8


def _count_le(sorted_vals, queries):
    return jnp.sum((sorted_vals[None, :] <= queries[:, None]).astype(jnp.int32), axis=1)


def _step_lists(cnt, fanin, n_steps):
    steps = jnp.maximum((cnt + fanin - 1) // fanin, 1)
    ends = jnp.cumsum(steps)
    w = jnp.arange(n_steps, dtype=jnp.int32)
    grp = jnp.minimum(_count_le(ends, w), cnt.shape[0] - 1)
    per_grp = jnp.stack([ends - steps, steps, cnt], axis=1)[grp]
    rank = w - per_grp[:, 0]
    valid = w < ends[-1]
    item0 = rank * fanin
    flags = (jnp.where(valid & (rank == 0), FLAG_FIRST, 0)
             | jnp.where(valid & (rank == per_grp[:, 1] - 1), FLAG_LAST, 0))
    return grp, item0, flags, jnp.where(valid, per_grp[:, 2] - item0, 0)


def _with_live_flags(flags, remaining, slot_ranks):
    for k, r in enumerate(slot_ranks):
        flags = flags | jnp.where(r < remaining, FLAG_SUB << k, 0)
    return flags.astype(jnp.int32)


def _route_tables(aff, n, cap, ts, tc, tt):
    gate, idx = lax.top_k(aff[:, :N_EXPERTS].T, cap)
    idx, gate = lax.sort((idx, gate), dimension=1, num_keys=1)
    idx = idx.astype(jnp.int32)

    tiles_per_e = cap // ts
    n_tiles = N_EXPERTS * tiles_per_e
    n_chunks = n // tc
    c0 = (idx[:, ::ts] // tc).reshape(n_tiles)
    c1 = (idx[:, ts - 1::ts] // tc).reshape(n_tiles)
    n_gs = (N_EXPERTS * n_chunks + n_tiles) // GATHER_FANIN + n_tiles
    g_tile, g_item0, g_flags, g_left = _step_lists(c1 - c0 + 1, GATHER_FANIN, n_gs)
    g_first = c0[g_tile] + g_item0
    g_ranks = [(k - g_first) % GATHER_FANIN for k in range(GATHER_FANIN)]
    g_flags = _with_live_flags(g_flags, g_left, g_ranks)
    g_chunks = jnp.concatenate([jnp.minimum(g_first + r, n_chunks - 1) for r in g_ranks])

    n_tt = n // tt
    blk_per_e = cap // Y_BLK
    bounds = jnp.arange(n_tt + 1, dtype=jnp.int32) * tt
    pos = jnp.sum((idx[:, None, :] < bounds[None, :, None]).astype(jnp.int32), axis=2)
    lo, hi = pos[:, :-1], pos[:, 1:]
    b0 = jnp.minimum(lo // Y_BLK, blk_per_e - 1)
    b1 = jnp.maximum((hi - 1) // Y_BLK, b0)
    pair_cnt = jnp.where(hi > lo, b1 - b0 + 1, 0).T.reshape(-1)
    pair_b0 = (b0 + (jnp.arange(N_EXPERTS, dtype=jnp.int32) * blk_per_e)[:, None]).T.reshape(-1)
    n_items = N_EXPERTS * blk_per_e + N_EXPERTS * n_tt
    item_ends = jnp.cumsum(pair_cnt)
    it = jnp.arange(n_items, dtype=jnp.int32)
    pair = jnp.minimum(_count_le(item_ends, it), pair_cnt.shape[0] - 1)
    item_blk = jnp.minimum(pair_b0[pair] + it - (item_ends - pair_cnt)[pair], N_EXPERTS * blk_per_e - 1)
    tile_cnt = jnp.sum(pair_cnt.reshape(n_tt, N_EXPERTS), axis=1)
    tile_item0 = jnp.cumsum(tile_cnt) - tile_cnt
    n_cs = n_items // COMBINE_FANIN + n_tt
    c_tile, c_item0, c_flags, c_left = _step_lists(tile_cnt, COMBINE_FANIN, n_cs)
    c_flags = _with_live_flags(c_flags, c_left, list(range(COMBINE_FANIN)))
    c_first = tile_item0[c_tile] + c_item0
    c_blks = jnp.concatenate([item_blk[jnp.minimum(c_first + k, n_items - 1)] for k in range(COMBINE_FANIN)])
    return idx, gate, (g_tile, g_chunks, g_flags), (c_tile, c_blks, c_flags)


def _ffn_kernel(tc, tile_ref, chunk_ref, flag_ref, tok_ref, gate_ref, *refs):
    hn_refs = refs[:GATHER_FANIN]
    wg_ref, wu_ref, wd_ref, y_ref, xacc = refs[GATHER_FANIN:]
    w = pl.program_id(0)
    n_steps = pl.num_programs(0)
    flags = flag_ref[w]

    @pl.when((flags & FLAG_FIRST) != 0)
    def _():
        xacc[...] = jnp.zeros(xacc.shape, F32)

    for k in range(GATHER_FANIN):
        @pl.when((flags & (FLAG_SUB << k)) != 0)
        def _(k=k):
            ts = tok_ref.shape[0]
            token = chunk_ref[k * n_steps + w] * tc + lax.broadcasted_iota(jnp.int32, (ts, tc), 1)
            onehot = jnp.where(tok_ref[...] == token, 1.0, 0.0).astype(BF16)
            xacc[...] += jnp.dot(onehot, hn_refs[k][...], preferred_element_type=F32)

    @pl.when((flags & FLAG_LAST) != 0)
    def _():
        xg = xacc[...].astype(BF16)
        a = jnp.dot(xg, wg_ref[0, 0], preferred_element_type=F32)
        b = jnp.dot(xg, wu_ref[0, 0], preferred_element_type=F32)
        hid = (a * jax.nn.sigmoid(a) * b).astype(BF16)
        y_ref[...] = (jnp.dot(hid, wd_ref[0, 0], preferred_element_type=F32) * gate_ref[...]).astype(BF16)


def _ffn_call(steps, idx, gate, hn, w_gate, w_up, w_down, layer, ts, tc):
    n_exp, cap = idx.shape
    tiles_per_e = cap // ts
    tile, chunks, flags = steps
    n_steps = tile.shape[0]
    tok = idx.reshape(n_exp * cap, 1)
    gate2 = gate.reshape(n_exp * cap, 1)
    slot = lambda wd: pl.BlockSpec((ts, wd), lambda w, tile, chunks, flags: (tile[w], 0))
    chunk = lambda k: pl.BlockSpec((tc, D_MODEL), lambda w, tile, chunks, flags: (chunks[k * n_steps + w], 0))
    wspec = pl.BlockSpec((1, 1, D_MODEL, D_MODEL), lambda w, tile, chunks, flags: (layer, tile[w] // tiles_per_e, 0, 0))
    grid_spec = pltpu.PrefetchScalarGridSpec(
        num_scalar_prefetch=3, grid=(n_steps,),
        in_specs=[slot(1), slot(1)] + [chunk(k) for k in range(GATHER_FANIN)] + [wspec, wspec, wspec],
        out_specs=slot(D_MODEL),
        scratch_shapes=[pltpu.VMEM((ts, D_MODEL), F32)])
    return pl.pallas_call(
        functools.partial(_ffn_kernel, tc), grid_spec=grid_spec,
        out_shape=jax.ShapeDtypeStruct((n_exp * cap, D_MODEL), BF16),
        compiler_params=_cparams(("arbitrary",)), name="expert_ffn",
    )(tile, chunks, flags, tok, gate2, *([hn] * GATHER_FANIN), w_gate, w_up, w_down)


def _combine_kernel(tt, final, tile_ref, blk_ref, flag_ref, *refs):
    tok_refs = refs[:COMBINE_FANIN]
    y_refs = refs[COMBINE_FANIN:2 * COMBINE_FANIN]
    x_ref, g_ref, o_ref = refs[2 * COMBINE_FANIN:]
    w = pl.program_id(0)
    flags = flag_ref[w]

    @pl.when((flags & FLAG_FIRST) != 0)
    def _():
        o_ref[...] = x_ref[...]

    @pl.when((flags & FLAG_SUB) != 0)
    def _():
        token = tile_ref[w] * tt + lax.broadcasted_iota(jnp.int32, (tt, Y_BLK), 0)
        hots = []
        for k in range(COMBINE_FANIN):
            live = (flags & (FLAG_SUB << k)) != 0
            tok = jnp.where(live, tok_refs[k][0], -1)
            hots.append(jnp.where(token == tok, 1.0, 0.0).astype(BF16))
        onehot = jnp.concatenate(hots, axis=1)
        ycat = jnp.concatenate([y_refs[k][...] for k in range(COMBINE_FANIN)], axis=0)
        o_ref[...] += jnp.dot(onehot, ycat, preferred_element_type=F32)

    if final:
        @pl.when((flags & FLAG_LAST) != 0)
        def _():
            o_ref[...] = _rms(o_ref[...]) * g_ref[...]


def _combine_call(steps, idx, y, x2, g_final, tt, final):
    n = x2.shape[0]
    tile, blks, flags = steps
    n_steps = tile.shape[0]
    tok = idx.reshape(-1, 1, Y_BLK)
    tspec = lambda k: pl.BlockSpec((1, 1, Y_BLK), lambda w, tile, blks, flags: (blks[k * n_steps + w], 0, 0))
    yspec = lambda k: pl.BlockSpec((Y_BLK, D_MODEL), lambda w, tile, blks, flags: (blks[k * n_steps + w], 0))
    xspec = pl.BlockSpec((tt, D_MODEL), lambda w, tile, blks, flags: (tile[w], 0))
    fan = range(COMBINE_FANIN)
    grid_spec = pltpu.PrefetchScalarGridSpec(
        num_scalar_prefetch=3, grid=(n_steps,),
        in_specs=[tspec(k) for k in fan] + [yspec(k) for k in fan]
        + [xspec, pl.BlockSpec((1, D_MODEL), lambda w, tile, blks, flags: (0, 0))],
        out_specs=xspec)
    return pl.pallas_call(
        functools.partial(_combine_kernel, tt, final), grid_spec=grid_spec,
        out_shape=jax.ShapeDtypeStruct((n, D_MODEL), F32),
        compiler_params=_cparams(("arbitrary",)), name="expert_combine",
    )(tile, blks, flags, *([tok] * COMBINE_FANIN), *([y] * COMBINE_FANIN), x2, g_final)


def _rot_cols(w, d):
    k, c = w.shape
    half = d // 2
    wg = w.reshape(k, c // d, 2, half)
    return jnp.concatenate([-wg[:, :, 1], wg[:, :, 0]], axis=2).reshape(k, c)


def _rope_tables(t_len):
    pos = jnp.arange(t_len, dtype=F32)

    def cs(d):
        half = d // 2
        inv = ROPE_THETA ** (-jnp.arange(half, dtype=F32) / half)
        ang = pos[:, None] * inv[None, :]
        return (jnp.concatenate([jnp.cos(ang)] * 2, axis=1), jnp.concatenate([jnp.sin(ang)] * 2, axis=1))

    c32, s32 = cs(A_ROPE)
    c64, s64 = cs(HEAD_DIM)
    ones = jnp.ones((t_len, A_NOPE), F32)
    zeros = jnp.zeros((t_len, A_NOPE), F32)
    pad = jnp.zeros((t_len, A_PAD - A_NOPE - A_ROPE), F32)
    tab = jnp.concatenate([ones, c32, pad, zeros, s32, pad,
                           jnp.tile(c32, (1, LANES // A_ROPE)), jnp.tile(s32, (1, LANES // A_ROPE)),
                           jnp.tile(c64, (1, LANES // HEAD_DIM)), jnp.tile(s64, (1, LANES // HEAD_DIM))], axis=1)
    tabt = jnp.concatenate([c32.T, s32.T], axis=0)
    return tab, tabt


def _layer_weights(l, w_in, a_w_uq, a_w_ukv, w_out, w_router):
    pts = np.cumsum(PROJ_SIZES)[:-1]
    (a_cq, a_ckv, a_kr, b_q, b_k, b_v, c_q, c_k, c_v, d_q, d_k, d_v) = jnp.split(w_in[l], pts, axis=1)
    wm = jnp.concatenate([a_cq, a_ckv, b_q, b_k, b_v, c_q, _rot_cols(c_q, C_DIM), c_v,
                          d_q, _rot_cols(d_q, HEAD_DIM), d_k, _rot_cols(d_k, HEAD_DIM), d_v], axis=1).astype(BF16)
    wt = jnp.concatenate([c_k, _rot_cols(c_k, C_DIM), a_kr, _rot_cols(a_kr, A_ROPE)], axis=1).T.astype(BF16)
    uq = a_w_uq[l].reshape(A_Q_LORA, N_HEADS, A_NOPE + A_ROPE)
    zpad = jnp.zeros((A_Q_LORA, N_HEADS, A_PAD - A_NOPE - A_ROPE), F32)
    uq_rope = uq[:, :, A_NOPE:]
    uq_rot = _rot_cols(uq_rope.reshape(A_Q_LORA, N_HEADS * A_ROPE), A_ROPE).reshape(A_Q_LORA, N_HEADS, A_ROPE)
    plain = jnp.concatenate([uq, zpad], axis=2).reshape(A_Q_LORA, N_HEADS * A_PAD)
    rot = jnp.concatenate([jnp.zeros_like(uq[:, :, :A_NOPE]), uq_rot, zpad], axis=2).reshape(A_Q_LORA, N_HEADS * A_PAD)
    wuq = jnp.concatenate([plain, rot], axis=1).astype(BF16)
    ukv = a_w_ukv[l].reshape(A_KV_LORA, N_HEADS, 2 * HEAD_DIM)
    wukt = ukv[:, :, :A_NOPE].reshape(A_KV_LORA, MIX_COLS).T.astype(BF16)
    wuv = ukv[:, :, A_NOPE:].reshape(A_KV_LORA, MIX_COLS).astype(BF16)
    wo = w_out[l].astype(BF16)
    wr = jnp.concatenate([w_router[l], jnp.zeros((D_MODEL, LANES - N_EXPERTS), F32)], axis=1)
    wr_hi = wr.astype(BF16)
    wr = jnp.stack([wr_hi, (wr - wr_hi.astype(F32)).astype(BF16)])
    return wm, wt, wuq, wukt, wuv, wo, wr


def _trunk(x, norm1_g, w_in, a_qnorm_g, a_w_uq, a_kvnorm_g, a_w_ukv, b_rpb, c_lambda, c_subln_g, w_out,
           norm2_g, w_router, w_gate, w_up, w_down, final_g):
    bsz, t_len, _ = x.shape
    n = bsz * t_len
    tm = 512
    tq = 512
    tk_mla, tk_diff = min(4096, t_len), (4096 if t_len <= 4096 else 2048)
    tq_dil = 256
    ts, tc, tt = 256, 1024, 1024
    depth = w_in.shape[0]
    cap = CAP_FACTOR * n // N_EXPERTS
    assert t_len % tk_mla == 0 and t_len % tk_diff == 0 and t_len >= tq_dil + 2 * D_REACH and t_len // QBLK >= 3
    assert cap % ts == 0 and n % tc == 0 and n % tt == 0
    tab, tabt = _rope_tables(t_len)
    dil_masks = _dil_masks(tq_dil)
    x2d = x.reshape(n, D_MODEL)
    g_final = final_g.reshape(1, D_MODEL)
    for l in range(depth):
        lam_init = 0.8 - 0.6 * math.exp(-0.3 * l)
        wm, wt, wuq, wukt, wuv, wo, wr = _layer_weights(l, w_in, a_w_uq, a_w_ukv, w_out, w_router)
        row = lambda g: g[l].reshape(1, -1)
        (qa, kat, va, qb, kb, vb, qc, kct, vc, qd, kd, vd) = _proj_call(
            x2d, t_len, row(norm1_g), wm, wt, row(a_qnorm_g), wuq, row(a_kvnorm_g), wukt, wuv, tab, tabt, tm)
        oa = _mla_call(qa, kat, va, bsz, t_len, tq, tk_mla)
        ob = _nbr_call(qb, kb, vb, _nbr_bias_tables(b_rpb[l], t_len), bsz, t_len)
        lp = c_lambda[l].astype(F32)
        lam = (jnp.exp(jnp.sum(lp[0] * lp[1])) - jnp.exp(jnp.sum(lp[2] * lp[3])) + lam_init).reshape(1, 1)
        g_sub = jnp.tile(c_subln_g[l], N_HEADS).reshape(1, MIX_COLS)
        oc = _diff_call(qc, kct, vc, lam, g_sub, lam_init, bsz, t_len, tq, tk_diff)
        od = _dil_call(qd, kd, vd, dil_masks, bsz, t_len, tq_dil)
        x2, hn, aff = _out_call(oa, ob, oc, od, x2d, wo, row(norm2_g), wr, tm)
        idx, gate, g_items, c_items = _route_tables(aff, n, cap, ts, tc, tt)
        y = _ffn_call(g_items, idx, gate, hn, w_gate, w_up, w_down, l, ts, tc)
        x2d = _combine_call(c_items, idx, y, x2, g_final, tt, l == depth - 1)
    return x2d.reshape(bsz, t_len, D_MODEL)


def kernel(x_prompt, x_sample, norm1_g, w_in, a_qnorm_g, a_w_uq, a_kvnorm_g, a_w_ukv, b_rpb, c_lambda, c_subln_g,
           w_out, norm2_g, w_router, w_gate, w_up, w_down, final_g):
    params = (norm1_g, w_in, a_qnorm_g, a_w_uq, a_kvnorm_g, a_w_ukv, b_rpb, c_lambda, c_subln_g, w_out,
              norm2_g, w_router, w_gate.astype(BF16), w_up.astype(BF16), w_down.astype(BF16), final_g)
    return (_trunk(x_prompt, *params), _trunk(x_sample, *params))
```
